```python
import jax, jax.numpy as jnp
from jax import lax
import numpy as np

D_MODEL = 2048
BATCH = 4
SEQ = 2048
DEPTH = 2
DEC_BATCH = 32
DEC_SEQ = 1
PAST_LEN = 8192
PAGE_SIZE = 128

HEAD_DIM = 128
MIX_DIM = D_MODEL
CONV_DIM = MIX_DIM // 4
SB_DIM = MIX_DIM // 2
RET_DIM = MIX_DIM // 4
CONV_HEADS = CONV_DIM // HEAD_DIM
SB_HEADS = SB_DIM // HEAD_DIM
RET_HEADS = RET_DIM // HEAD_DIM
CONV_WIDTH = 3
IN_COLS = 3 * CONV_DIM + 3 * SB_DIM + 4 * RET_DIM
FFN_DIM = -(-8 * D_MODEL // (3 * 256)) * 256
SB_BLOCK = 128
RET_CHUNK = 128
ROPE_BASE = 10000.0
RMS_EPS = 1e-6
SB_BIAS_NEAR = 2.0
SB_BIAS_FAR = 10.0

kernel_name = 'hybrid_conv_stickbreak_retention_step'


def rms_norm(x, gain):
    xf = x.astype(jnp.float32)
    y = xf * lax.rsqrt(jnp.mean(xf * xf, axis=-1, keepdims=True) + RMS_EPS)
    return (y * gain.astype(jnp.float32)).astype(x.dtype)


def head_rms(x):
    xf = x.astype(jnp.float32)
    return (xf * lax.rsqrt(jnp.mean(xf * xf, axis=-1, keepdims=True) + RMS_EPS)).astype(x.dtype)


def split_heads(t):
    return t.reshape(t.shape[0], t.shape[1], -1, HEAD_DIM)


def split_proj(proj):
    widths = (CONV_DIM,) * 3 + (SB_DIM,) * 3 + (RET_DIM,) * 4
    idx = []
    acc = 0
    for w in widths[:-1]:
        acc += w
        idx.append(acc)
    return jnp.split(proj, idx, axis=-1)


def short_conv(u, state, w):
    L = u.shape[1]
    full = jnp.concatenate([state.astype(u.dtype), u], axis=1)
    y = full[:, 0:L] * w[0]
    for i in range(1, CONV_WIDTH):
        y = y + full[:, i:i + L] * w[i]
    return y, full[:, L:]


def rotary(x, pos):
    inv = ROPE_BASE ** (-jnp.arange(0, HEAD_DIM, 2, dtype=jnp.float32) / HEAD_DIM)
    ang = pos.astype(jnp.float32)[:, None] * inv[None, :]
    cos = jnp.cos(ang)[None, :, None, :]
    sin = jnp.sin(ang)[None, :, None, :]
    x1, x2 = jnp.split(x, 2, axis=-1)
    return jnp.concatenate([x1 * cos - x2 * sin, x1 * sin + x2 * cos], axis=-1)


def stick_breaking(q, k, v, q_pos, bias):
    T = k.shape[1]
    z = jnp.einsum('bqhd,bkhd->bhqk', q.astype(jnp.float32), k.astype(jnp.float32)) * (HEAD_DIM ** -0.5)
    z = z + bias.astype(jnp.float32)[None, :, None, None]
    mask = (jnp.arange(T)[None, :] < q_pos[:, None])[None, None]
    log_rem = jnp.where(mask, jax.nn.log_sigmoid(-z), 0.0)
    later = lax.cumsum(log_rem, axis=3, reverse=True) - log_rem
    a = jnp.where(mask, jnp.exp(jax.nn.log_sigmoid(z) + later), 0.0)
    return jnp.einsum('bhqk,bkhd->bqhd', a, v.astype(jnp.float32)).astype(v.dtype)


def sb_prompt(q, k, v, bias):
    B, L, H, Dh = q.shape
    nb = L // SB_BLOCK
    qb = q.reshape(B, nb, SB_BLOCK, H, Dh).transpose(1, 0, 2, 3, 4)
    starts = jnp.arange(nb, dtype=jnp.int32) * SB_BLOCK

    def one_block(args):
        qblk, s0 = args
        return stick_breaking(qblk, k, v, s0 + jnp.arange(SB_BLOCK, dtype=jnp.int32), bias)

    out = lax.map(one_block, (qb, starts))
    return out.transpose(1, 0, 2, 3, 4).reshape(B, L, H, Dh)


def log_gamma():
    return jnp.log1p(-jnp.exp2(-5.0 - jnp.arange(RET_HEADS, dtype=jnp.float32)))


def retention_chunk(S, chunk):
    q, k, v = chunk
    C = q.shape[1]
    lg = log_gamma()
    n = jnp.arange(C, dtype=jnp.float32)
    diff = n[:, None] - n[None, :]
    dmat = jnp.where(diff >= 0, jnp.exp(jnp.maximum(diff, 0.0)[None] * lg[:, None, None]), 0.0)
    inner = jnp.einsum('bnhd,bmhd->bhnm', q, k) * dmat[None]
    cross_decay = jnp.exp((n + 1.0)[:, None] * lg[None, :])[None, :, :, None]
    o = jnp.einsum('bhnm,bmhe->bnhe', inner, v) + jnp.einsum('bnhd,bhde->bnhe', q, S) * cross_decay
    k_dec = k * jnp.exp((C - 1.0 - n)[:, None] * lg[None, :])[None, :, :, None]
    S_new = jnp.exp(C * lg)[None, :, None, None] * S + jnp.einsum('bmhd,bmhe->bhde', k_dec, v)
    return S_new, o


def retention(q, k, v, S0):
    B, L, H, Dh = q.shape
    C = RET_CHUNK if L % RET_CHUNK == 0 else L
    nc = L // C

    def to_chunks(t):
        return t.reshape(B, nc, C, H, Dh).transpose(1, 0, 2, 3, 4)

    S_final, o = lax.scan(retention_chunk, S0, (to_chunks(q), to_chunks(k), to_chunks(v)))
    return o.transpose(1, 0, 2, 3, 4).reshape(B, L, H, Dh), S_final


def mixer(h, pos0, conv_state, ret_state, k_past, v_past, w_in, conv_w, sb_bias, sb_gain, w_out):
    B, L, _ = h.shape
    pos = pos0 + jnp.arange(L, dtype=jnp.int32)
    cb, cc, ch, sq, sk, sv, rq, rk, rv, rg = split_proj(h @ w_in)
    conv_y, conv_new = short_conv(cc * ch, conv_state, conv_w)
    conv_out = cb * conv_y
    q, k, v = split_heads(sq), split_heads(sk), split_heads(sv)
    if k_past is None:
        sb = sb_prompt(q, k, v, sb_bias)
    else:
        k_all = jnp.concatenate([k_past.astype(k.dtype), k], axis=1)
        v_all = jnp.concatenate([v_past.astype(v.dtype), v], axis=1)
        sb = stick_breaking(q, k_all, v_all, pos, sb_bias)
    sb_out = head_rms(sb).reshape(B, L, SB_DIM) * sb_gain
    rqh = rotary(split_heads(rq).astype(jnp.float32), pos)
    rkh = rotary(split_heads(rk).astype(jnp.float32), pos) * (HEAD_DIM ** -0.5)
    ro, S_new = retention(rqh, rkh, split_heads(rv).astype(jnp.float32), ret_state.astype(jnp.float32))
    ret_out = jax.nn.silu(rg) * head_rms(ro).reshape(B, L, RET_DIM).astype(h.dtype)
    mix = jnp.concatenate([conv_out, sb_out, ret_out], axis=-1) @ w_out
    return mix, k, v, conv_new, S_new


def layer(x, pos0, conv_state, ret_state, k_past, v_past,
          g_mix_pre, g_mix_post, g_ffn_pre, g_ffn_post,
          w_in, conv_w, sb_bias, sb_gain, w_out, w_gate_up, w_down):
    m, k, v, conv_new, S_new = mixer(rms_norm(x, g_mix_pre), pos0, conv_state, ret_state,
                                     k_past, v_past, w_in, conv_w, sb_bias, sb_gain, w_out)
    x = x + rms_norm(m, g_mix_post)
    gate, up = jnp.split(rms_norm(x, g_ffn_pre) @ w_gate_up, 2, axis=-1)
    x = x + rms_norm((jax.nn.silu(gate) * up) @ w_down, g_ffn_post)
    return x, k, v, conv_new, S_new


def setup_inputs(seed: int = 0) -> dict:
    key = jax.random.key(seed)
    ks = jax.random.split(key, 20)
    f32 = jnp.float32
    n_pages = PAST_LEN // PAGE_SIZE
    n_used = DEC_BATCH * n_pages
    n_pool = n_used + max(1, n_used // 4)
    page_table = jax.random.permutation(ks[6], n_pool)[:n_used].reshape(DEC_BATCH, n_pages).astype(jnp.int32)

    def gain(k, n):
        return 1.0 + 0.02 * jax.random.normal(k, (DEPTH, n), f32)

    sb_bias = (-jnp.linspace(SB_BIAS_NEAR, SB_BIAS_FAR, SB_HEADS, dtype=f32)[None, :]
               + 0.1 * jax.random.normal(ks[17], (DEPTH, SB_HEADS), f32))
    return {
        'x_prompt': jax.random.normal(ks[0], (BATCH, SEQ, D_MODEL), f32),
        'x_sample': jax.random.normal(ks[1], (DEC_BATCH, DEC_SEQ, D_MODEL), f32),
        'cache_k': jax.random.normal(ks[2], (DEPTH, n_pool, PAGE_SIZE, SB_HEADS, HEAD_DIM), f32),
        'cache_v': jax.random.normal(ks[3], (DEPTH, n_pool, PAGE_SIZE, SB_HEADS, HEAD_DIM), f32),
        'state_conv': jax.random.normal(ks[4], (DEPTH, DEC_BATCH, CONV_WIDTH - 1, CONV_DIM), f32),
        'state_ret': 0.5 * jax.random.normal(ks[5], (DEPTH, DEC_BATCH, RET_HEADS, HEAD_DIM, HEAD_DIM), f32),
        'page_table': page_table,
        'norm_mix_pre': gain(ks[7], D_MODEL),
        'norm_mix_post': gain(ks[8], D_MODEL),
        'norm_ffn_pre': gain(ks[9], D_MODEL),
        'norm_ffn_post': gain(ks[10], D_MODEL),
        'w_in': jax.random.normal(ks[11], (DEPTH, D_MODEL, IN_COLS), f32) * D_MODEL ** -0.5,
        'conv_w': jax.random.normal(ks[12], (DEPTH, CONV_WIDTH, CONV_DIM), f32) * CONV_WIDTH ** -0.5,
        'sb_bias': sb_bias,
        'sb_gain': gain(ks[13], SB_DIM),
        'w_out': jax.random.normal(ks[14], (DEPTH, MIX_DIM, D_MODEL), f32) * MIX_DIM ** -0.5,
        'w_gate_up': jax.random.normal(ks[15], (DEPTH, D_MODEL, 2 * FFN_DIM), f32) * D_MODEL ** -0.5,
        'w_down': jax.random.normal(ks[16], (DEPTH, FFN_DIM, D_MODEL), f32) * FFN_DIM ** -0.5,
    }


def reference(x_prompt, x_sample, cache_k, cache_v, state_conv, state_ret, page_table,
              norm_mix_pre, norm_mix_post, norm_ffn_pre, norm_ffn_post,
              w_in, conv_w, sb_bias, sb_gain, w_out, w_gate_up, w_down):
    n_prompt = x_prompt.shape[0]
    n_dec = x_sample.shape[0]
    past_len = page_table.shape[1] * PAGE_SIZE
    xp, xs = x_prompt, x_sample
    kp_l, vp_l, cp_l, rp_l = [], [], [], []
    ks_l, vs_l, cs_l, rs_l = [], [], [], []
    for l in range(DEPTH):
        params = (norm_mix_pre[l], norm_mix_post[l], norm_ffn_pre[l], norm_ffn_post[l],
                  w_in[l], conv_w[l], sb_bias[l], sb_gain[l], w_out[l], w_gate_up[l], w_down[l])
        conv0 = jnp.zeros((n_prompt, CONV_WIDTH - 1, CONV_DIM), xp.dtype)
        ret0 = jnp.zeros((n_prompt, RET_HEADS, HEAD_DIM, HEAD_DIM), jnp.float32)
        xp, kp, vp, cp, rp = layer(xp, 0, conv0, ret0, None, None, *params)
        k_past = cache_k[l][page_table].reshape(n_dec, past_len, SB_HEADS, HEAD_DIM)
        v_past = cache_v[l][page_table].reshape(n_dec, past_len, SB_HEADS, HEAD_DIM)
        xs, ksn, vsn, csn, rsn = layer(xs, past_len, state_conv[l], state_ret[l], k_past, v_past, *params)
        kp_l.append(kp); vp_l.append(vp); cp_l.append(cp); rp_l.append(rp)
        ks_l.append(ksn); vs_l.append(vsn); cs_l.append(csn); rs_l.append(rsn)
    return (xp, xs,
            jnp.stack(kp_l), jnp.stack(vp_l), jnp.stack(cp_l), jnp.stack(rp_l),
            jnp.stack(ks_l), jnp.stack(vs_l), jnp.stack(cs_l), jnp.stack(rs_l))
```

```python
import functools

import jax
import jax.numpy as jnp
from jax import lax
from jax.experimental import pallas as pl
from jax.experimental.pallas import tpu as pltpu

D_MODEL = 2048
HEAD_DIM = 128
CONV_DIM = 512
SB_DIM = 1024
RET_DIM = 512
SB_HEADS = 8
RET_HEADS = 4
CONV_WIDTH = 3
IN_COLS = 3 * CONV_DIM + 3 * SB_DIM + 4 * RET_DIM
FFN_DIM = 5632
PAGE_SIZE = 128
ROPE_BASE = 10000.0
RMS_EPS = 1e-6
SCALE = HEAD_DIM ** -0.5

COL_CB, COL_CC, COL_CH = 0, 4, 8
COL_SQ, COL_SK, COL_SV = 12, 20, 28
COL_RQ, COL_RK, COL_RV, COL_RG = 36, 40, 44, 48

VMEM_LIMIT_BYTES = 56 * 1024 * 1024
BF16 = jnp.bfloat16
F32 = jnp.float32


def _params(*semantics):
    return pltpu.CompilerParams(dimension_semantics=semantics, vmem_limit_bytes=VMEM_LIMIT_BYTES)


def _rms(x, gain):
    return x * lax.rsqrt(jnp.mean(x * x, axis=-1, keepdims=True) + RMS_EPS) * gain


def _softplus(z):
    return jnp.maximum(z, 0.0) + jnp.log1p(jnp.exp(-jnp.abs(z)))


def _dot(a, b):
    return jnp.dot(a, b, preferred_element_type=F32)


def _dot_nt(a, b):
    return lax.dot_general(a, b, (((1,), (1,)), ((), ())), preferred_element_type=F32)


def _suffix_matrix():
    r = lax.broadcasted_iota(jnp.int32, (HEAD_DIM, 2 * HEAD_DIM), 0)
    c = lax.broadcasted_iota(jnp.int32, (HEAD_DIM, 2 * HEAD_DIM), 1)
    return jnp.where((r > c) | (c >= HEAD_DIM), 1.0, 0.0).astype(BF16)


def _suffix_and_total(log_rem, tri):
    n = log_rem.shape[0]
    hi = log_rem.astype(BF16)
    lo = (log_rem - hi.astype(F32)).astype(BF16)
    s = _dot(jnp.concatenate([hi, lo], axis=0), tri)
    s = s[:n] + s[n:]
    return s[:, :HEAD_DIM], s[:, HEAD_DIM:]


def _norm_cast_kernel(x_ref, g_ref, o_ref):
    o_ref[...] = _rms(x_ref[...], g_ref[...]).astype(BF16)


def norm_cast(x, gain, tm):
    m = x.shape[0]
    return pl.pallas_call(
        _norm_cast_kernel,
        grid=(m // tm,),
        in_specs=[pl.BlockSpec((tm, D_MODEL), lambda i: (i, 0)),
                  pl.BlockSpec((1, D_MODEL), lambda i: (0, 0))],
        out_specs=pl.BlockSpec((tm, D_MODEL), lambda i: (i, 0)),
        out_shape=jax.ShapeDtypeStruct((m, D_MODEL), BF16),
        compiler_params=_params("parallel"),
        name="norm_cast",
    )(x, gain.reshape(1, D_MODEL))


def _in_proj_kernel(a_ref, w_ref, o_ref):
    o_ref[...] = _dot(a_ref[...], w_ref[...])


def in_proj(a, w, tm, tn):
    m, k = a.shape
    n = w.shape[1]
    return pl.pallas_call(
        _in_proj_kernel,
        grid=(m // tm, n // tn),
        in_specs=[pl.BlockSpec((tm, k), lambda i, j: (i, 0)),
                  pl.BlockSpec((k, tn), lambda i, j: (0, j))],
        out_specs=pl.BlockSpec((tm, tn), lambda i, j: (i, j)),
        out_shape=jax.ShapeDtypeStruct((m, n), F32),
        compiler_params=_params("parallel", "arbitrary"),
        name="in_proj",
    )(a, w)


def _out_proj_kernel(c_ref, s_ref, r_ref, w_ref, x_ref, gpost_ref, gnext_ref, x_out_ref, xn_out_ref):
    mix = jnp.concatenate([c_ref[...], s_ref[...], r_ref[...]], axis=-1)
    x = x_ref[...] + _rms(_dot(mix, w_ref[...]), gpost_ref[...])
    x_out_ref[...] = x
    xn_out_ref[...] = _rms(x, gnext_ref[...]).astype(BF16)


def out_proj(conv_out, sb_out, ret_out, w, x, g_post, g_next, tm):
    m = x.shape[0]
    row = lambda i: (i, 0)
    fixed = lambda i: (0, 0)
    return pl.pallas_call(
        _out_proj_kernel,
        grid=(m // tm,),
        in_specs=[pl.BlockSpec((tm, CONV_DIM), row),
                  pl.BlockSpec((tm, SB_DIM), row),
                  pl.BlockSpec((tm, RET_DIM), row),
                  pl.BlockSpec((D_MODEL, D_MODEL), fixed),
                  pl.BlockSpec((tm, D_MODEL), row),
                  pl.BlockSpec((1, D_MODEL), fixed),
                  pl.BlockSpec((1, D_MODEL), fixed)],
        out_specs=[pl.BlockSpec((tm, D_MODEL), row), pl.BlockSpec((tm, D_MODEL), row)],
        out_shape=[jax.ShapeDtypeStruct((m, D_MODEL), F32), jax.ShapeDtypeStruct((m, D_MODEL), BF16)],
        compiler_params=_params("parallel"),
        name="out_proj",
    )(conv_out, sb_out, ret_out, w, x, g_post.reshape(1, -1), g_next.reshape(1, -1))


def _gate_up_kernel(a_ref, wg_ref, wu_ref, o_ref):
    a = a_ref[...]
    gate = _dot(a, wg_ref[...])
    up = _dot(a, wu_ref[...])
    o_ref[...] = (gate * jax.nn.sigmoid(gate) * up).astype(BF16)


def gate_up(a, w, tm, tf):
    m = a.shape[0]
    nf = FFN_DIM // tf
    return pl.pallas_call(
        _gate_up_kernel,
        grid=(m // tm, nf),
        in_specs=[pl.BlockSpec((tm, D_MODEL), lambda i, j: (i, 0)),
                  pl.BlockSpec((D_MODEL, tf), lambda i, j: (0, j)),
                  pl.BlockSpec((D_MODEL, tf), lambda i, j: (0, j + nf))],
        out_specs=pl.BlockSpec((tm, tf), lambda i, j: (i, j)),
        out_shape=jax.ShapeDtypeStruct((m, FFN_DIM), BF16),
        compiler_params=_params("parallel", "arbitrary"),
        name="gate_up",
    )(a, w, w)


def _down_kernel(emit_next, h_ref, w_ref, x_ref, gpost_ref, *rest):
    if emit_next:
        gnext_ref, x_out_ref, xn_out_ref, acc_ref = rest
    else:
        x_out_ref, acc_ref = rest
    k = pl.program_id(1)

    @pl.when(k == 0)
    def _():
        acc_ref[...] = jnp.zeros_like(acc_ref)

    acc_ref[...] += _dot(h_ref[...], w_ref[...])

    @pl.when(k == pl.num_programs(1) - 1)
    def _():
        x = x_ref[...] + _rms(acc_ref[...], gpost_ref[...])
        x_out_ref[...] = x
        if emit_next:
            xn_out_ref[...] = _rms(x, gnext_ref[...]).astype(BF16)


def down_proj(h, w, x, g_post, g_next, tm, tk):
    m = x.shape[0]
    emit_next = g_next is not None
    row = lambda i, k: (i, 0)
    fixed = lambda i, k: (0, 0)
    in_specs = [pl.BlockSpec((tm, tk), lambda i, k: (i, k)),
                pl.BlockSpec((tk, D_MODEL), lambda i, k: (k, 0)),
                pl.BlockSpec((tm, D_MODEL), row),
                pl.BlockSpec((1, D_MODEL), fixed)]
    args = [h, w, x, g_post.reshape(1, -1)]
    out_specs = [pl.BlockSpec((tm, D_MODEL), row)]
    out_shape = [jax.ShapeDtypeStruct((m, D_MODEL), F32)]
    if emit_next:
        in_specs.append(pl.BlockSpec((1, D_MODEL), fixed))
        args.append(g_next.reshape(1, -1))
        out_specs.append(pl.BlockSpec((tm, D_MODEL), row))
        out_shape.append(jax.ShapeDtypeStruct((m, D_MODEL), BF16))
    outs = pl.pallas_call(
        functools.partial(_down_kernel, emit_next),
        grid=(m // tm, FFN_DIM // tk),
        in_specs=in_specs,
        out_specs=out_specs,
        out_shape=out_shape,
        scratch_shapes=[pltpu.VMEM((tm, D_MODEL), F32)],
        compiler_params=_params("parallel", "arbitrary"),
        name="down_proj",
    )(*args)
    return (outs[0], outs[1]) if emit_next else (outs[0], None)


def _conv_prompt_kernel(cb_ref, cc_ref, ch_ref, st_ref, w_ref, o_ref, new_ref):
    u = cc_ref[0] * ch_ref[0]
    seq = u.shape[0]
    st = st_ref[0]
    w = w_ref[...]
    row = lax.broadcasted_iota(jnp.int32, u.shape, 0)
    back1 = jnp.where(row == 0, st[1:2], pltpu.roll(u, 1, 0))
    back2 = jnp.where(row == 0, st[0:1], jnp.where(row == 1, st[1:2], pltpu.roll(u, 2, 0)))
    y = back2 * w[0:1] + back1 * w[1:2] + u * w[2:3]
    o_ref[0] = (cb_ref[0] * y).astype(BF16)
    new_ref[0] = u[seq - 2:, :]


def conv_prompt(proj3, state, conv_w):
    b, seq, _ = proj3.shape
    col = lambda off: pl.BlockSpec((1, seq, HEAD_DIM), lambda i, c: (i, 0, off + c))
    return pl.pallas_call(
        _conv_prompt_kernel,
        grid=(b, CONV_DIM // HEAD_DIM),
        in_specs=[col(COL_CB), col(COL_CC), col(COL_CH),
                  pl.BlockSpec((1, CONV_WIDTH - 1, HEAD_DIM), lambda i, c: (i, 0, c)),
                  pl.BlockSpec((CONV_WIDTH, HEAD_DIM), lambda i, c: (0, c))],
        out_specs=[pl.BlockSpec((1, seq, HEAD_DIM), lambda i, c: (i, 0, c)),
                   pl.BlockSpec((1, CONV_WIDTH - 1, HEAD_DIM), lambda i, c: (i, 0, c))],
        out_shape=[jax.ShapeDtypeStruct((b, seq, CONV_DIM), BF16),
                   jax.ShapeDtypeStruct((b, CONV_WIDTH - 1, CONV_DIM), F32)],
        compiler_params=_params("parallel", "parallel"),
        name="conv_prompt",
    )(proj3, proj3, proj3, state, conv_w)


def _conv_step_kernel(cb_ref, cc_ref, ch_ref, s0_ref, s1_ref, w_ref, o_ref, u_ref):
    u = cc_ref[...] * ch_ref[...]
    w = w_ref[...]
    y = s0_ref[...] * w[0:1] + s1_ref[...] * w[1:2] + u * w[2:3]
    o_ref[...] = (cb_ref[...] * y).astype(BF16)
    u_ref[...] = u


def conv_step(proj, state, conv_w):
    n = proj.shape[0]
    col = lambda j: pl.BlockSpec((n, CONV_DIM), lambda i: (0, j))
    full = pl.BlockSpec((n, CONV_DIM), lambda i: (0, 0))
    out, u = pl.pallas_call(
        _conv_step_kernel,
        grid=(1,),
        in_specs=[col(0), col(1), col(2), full, full,
                  pl.BlockSpec((CONV_WIDTH, CONV_DIM), lambda i: (0, 0))],
        out_specs=[full, full],
        out_shape=[jax.ShapeDtypeStruct((n, CONV_DIM), BF16), jax.ShapeDtypeStruct((n, CONV_DIM), F32)],
        compiler_params=_params("arbitrary"),
        name="conv_step",
    )(proj, proj, proj, state[:, 0], state[:, 1], conv_w)
    return out, jnp.stack([state[:, 1], u], axis=1)


def _sb_prompt_kernel(q_ref, k_ref, v_ref, bias_ref, gain_ref, o_ref):
    qi = pl.program_id(2)
    tq = q_ref.shape[1]
    q = q_ref[0].astype(BF16)
    bias = bias_ref[0]
    tri = _suffix_matrix()
    row = lax.broadcasted_iota(jnp.int32, (tq, HEAD_DIM), 0)
    col = lax.broadcasted_iota(jnp.int32, (tq, HEAD_DIM), 1)

    def body(t, carry):
        acc, later_blocks = carry
        j = qi - t
        start = pl.multiple_of(j * HEAD_DIM, HEAD_DIM)
        kb = k_ref[0, pl.ds(start, HEAD_DIM), :].astype(BF16)
        vb = v_ref[0, pl.ds(start, HEAD_DIM), :].astype(BF16)
        z = _dot_nt(q, kb) * SCALE + bias
        mask = (col + j * HEAD_DIM) < (row + qi * tq)
        sp = _softplus(z)
        log_rem = jnp.where(mask, -sp, 0.0)
        suffix, total = _suffix_and_total(log_rem, tri)
        a = jnp.where(mask, jnp.exp(z - sp + suffix + later_blocks), 0.0)
        return acc + _dot(a.astype(BF16), vb), later_blocks + total

    zero = jnp.zeros((tq, HEAD_DIM), F32)
    acc, _ = lax.fori_loop(0, qi + 1, body, (zero, zero))
    o_ref[0] = (_rms(acc, 1.0) * gain_ref[0]).astype(BF16)


def sb_prompt(proj3, bias, gain):
    b, seq, _ = proj3.shape
    tq = HEAD_DIM
    kv = lambda off: pl.BlockSpec((1, seq, HEAD_DIM), lambda i, h, q: (i, 0, off + h))
    per_head = pl.BlockSpec((1, 1, HEAD_DIM), lambda i, h, q: (h, 0, 0))
    return pl.pallas_call(
        _sb_prompt_kernel,
        grid=(b, SB_HEADS, seq // tq),
        in_specs=[pl.BlockSpec((1, tq, HEAD_DIM), lambda i, h, q: (i, q, COL_SQ + h)),
                  kv(COL_SK), kv(COL_SV), per_head, per_head],
        out_specs=pl.BlockSpec((1, tq, HEAD_DIM), lambda i, h, q: (i, q, h)),
        out_shape=jax.ShapeDtypeStruct((b, seq, SB_DIM), BF16),
        compiler_params=_params("parallel", "parallel", "arbitrary"),
        name="sb_prompt",
    )(proj3, proj3, proj3,
      jnp.broadcast_to(bias[:, None, None], (SB_HEADS, 1, HEAD_DIM)),
      gain.reshape(SB_HEADS, 1, HEAD_DIM))


def _sb_decode_kernel(pt_ref, q_ref, k_ref, v_ref, bias_ref, gain_ref, o_ref, acc_ref, later_ref):
    del pt_ref
    t = pl.program_id(1)

    @pl.when(t == 0)
    def _():
        acc_ref[...] = jnp.zeros_like(acc_ref)
        later_ref[...] = jnp.zeros_like(later_ref)

    q = q_ref[0].astype(BF16)
    head = lax.broadcasted_iota(jnp.int32, (SB_HEADS, HEAD_DIM), 0)
    z = jnp.zeros((SB_HEADS, HEAD_DIM), F32)
    for h in range(SB_HEADS):
        kh = k_ref[0, 0, pl.ds(h, PAGE_SIZE, stride=SB_HEADS), :].astype(BF16)
        z = jnp.where(head == h, _dot_nt(q, kh), z)
    z = z * SCALE + bias_ref[...]
    sp = _softplus(z)
    suffix, total = _suffix_and_total(-sp, _suffix_matrix())
    a = jnp.exp(z - sp + suffix + later_ref[...]).astype(BF16)
    later_ref[...] += total
    acc = acc_ref[...]
    for h in range(SB_HEADS):
        vh = v_ref[0, 0, pl.ds(h, PAGE_SIZE, stride=SB_HEADS), :].astype(BF16)
        acc = jnp.where(head == h, acc + _dot(a, vh), acc)
    acc_ref[...] = acc

    @pl.when(t == pl.num_programs(1) - 1)
    def _():
        o_ref[0] = _rms(acc, 1.0) * gain_ref[...]


def sb_decode(q, cache_k, cache_v, layer, page_table, bias, gain):
    n, n_pages = page_table.shape
    rows = PAGE_SIZE * SB_HEADS
    depth, n_pool = cache_k.shape[:2]
    ck = cache_k.reshape(depth, n_pool, rows, HEAD_DIM)
    cv = cache_v.reshape(depth, n_pool, rows, HEAD_DIM)
    page = pl.BlockSpec((1, 1, rows, HEAD_DIM),
                        lambda i, t, pt: (layer, pt[i, n_pages - 1 - t], 0, 0))
    per_head = pl.BlockSpec((SB_HEADS, HEAD_DIM), lambda i, t, pt: (0, 0))
    seq_block = pl.BlockSpec((1, SB_HEADS, HEAD_DIM), lambda i, t, pt: (i, 0, 0))
    return pl.pallas_call(
        _sb_decode_kernel,
        grid_spec=pltpu.PrefetchScalarGridSpec(
            num_scalar_prefetch=1,
            grid=(n, n_pages),
            in_specs=[seq_block, page, page, per_head, per_head],
            out_specs=seq_block,
            scratch_shapes=[pltpu.VMEM((SB_HEADS, HEAD_DIM), F32), pltpu.VMEM((SB_HEADS, HEAD_DIM), F32)]),
        out_shape=jax.ShapeDtypeStruct((n, SB_HEADS, HEAD_DIM), F32),
        compiler_params=_params("parallel", "arbitrary"),
        name="sb_decode",
    )(page_table, q, ck, cv,
      jnp.broadcast_to(bias[:, None], (SB_HEADS, HEAD_DIM)),
      gain.reshape(SB_HEADS, HEAD_DIM))


def _rotate(x, cos2, sin2):
    return x * cos2 + pltpu.roll(x, HEAD_DIM // 2, 1) * sin2


def _retention_kernel(chunk, n_chunks, q_ref, k_ref, v_ref, g_ref, cos_ref, sin_ref, lg_ref, s_ref,
                      o_ref, s_out_ref):
    lg = lg_ref[0]
    r = lax.broadcasted_iota(jnp.int32, (HEAD_DIM, HEAD_DIM), 0).astype(F32)
    c = lax.broadcasted_iota(jnp.int32, (HEAD_DIM, HEAD_DIM), 1).astype(F32)
    diff = r - c
    dmat = jnp.where(diff >= 0, jnp.exp(jnp.maximum(diff, 0.0) * lg), 0.0)
    cross = jnp.exp((r + 1.0) * lg)
    k_decay = jnp.exp((chunk - 1.0 - r) * lg)
    s_decay = jnp.exp(chunk * lg)

    def body(i, s):
        start = pl.multiple_of(i * HEAD_DIM, HEAD_DIM)
        rows = pl.ds(start, HEAD_DIM)
        cos2 = cos_ref[rows, :]
        sin2 = sin_ref[rows, :]
        q = _rotate(q_ref[0, rows, :], cos2, sin2).astype(BF16)
        k = _rotate(k_ref[0, rows, :], cos2, sin2) * SCALE
        v = v_ref[0, rows, :].astype(BF16)
        inner = _dot_nt(q, k.astype(BF16)) * dmat
        o = _dot(inner.astype(BF16), v) + _dot(q, s.astype(BF16)) * cross
        g = g_ref[0, rows, :]
        o_ref[0, rows, :] = (g * jax.nn.sigmoid(g) * _rms(o, 1.0)).astype(o_ref.dtype)
        return s_decay * s + _dot((k * k_decay).T.astype(BF16), v)

    s_out_ref[0, 0] = lax.fori_loop(0, n_chunks, body, s_ref[0, 0])


def retention(proj3, col_q, cos2, sin2, log_gamma, state, chunk):
    b, rows, _ = proj3.shape
    n_chunks = rows // HEAD_DIM
    col = lambda off: pl.BlockSpec((1, rows, HEAD_DIM), lambda i, h: (i, 0, col_q + off + h))
    table = pl.BlockSpec((rows, HEAD_DIM), lambda i, h: (0, 0))
    st = pl.BlockSpec((1, 1, HEAD_DIM, HEAD_DIM), lambda i, h: (i, h, 0, 0))
    return pl.pallas_call(
        functools.partial(_retention_kernel, float(chunk), n_chunks),
        grid=(b, RET_HEADS),
        in_specs=[col(0), col(RET_HEADS), col(2 * RET_HEADS), col(3 * RET_HEADS), table, table,
                  pl.BlockSpec((1, 1, HEAD_DIM), lambda i, h: (h, 0, 0)), st],
        out_specs=[pl.BlockSpec((1, rows, HEAD_DIM), lambda i, h: (i, 0, h)), st],
        out_shape=[jax.ShapeDtypeStruct((b, rows, RET_DIM), BF16),
                   jax.ShapeDtypeStruct((b, RET_HEADS, HEAD_DIM, HEAD_DIM), F32)],
        compiler_params=_params("parallel", "parallel"),
        name="retention",
    )(proj3, proj3, proj3, proj3, cos2, sin2,
      jnp.broadcast_to(log_gamma[:, None, None], (RET_HEADS, 1, HEAD_DIM)), state)


def _rotary_tables(pos):
    inv = ROPE_BASE ** (-jnp.arange(0, HEAD_DIM, 2, dtype=F32) / HEAD_DIM)
    ang = pos.astype(F32)[:, None] * inv[None, :]
    cos, sin = jnp.cos(ang), jnp.sin(ang)
    return jnp.concatenate([cos, cos], axis=-1), jnp.concatenate([-sin, sin], axis=-1)


def _layer(group, x, xn, cache, page_table, p, g_next):
    is_prompt = group[0] == "prompt"
    m = x.shape[0]
    tm = 512 if is_prompt else m
    proj = in_proj(xn, p["w_in"], tm, 1664)
    log_gamma = jnp.log1p(-jnp.exp2(-5.0 - jnp.arange(RET_HEADS, dtype=F32)))
    if is_prompt:
        _, b, seq = group
        proj3 = proj.reshape(b, seq, IN_COLS)
        conv_out, conv_new = conv_prompt(proj3, jnp.zeros((b, CONV_WIDTH - 1, CONV_DIM), F32), p["conv_w"])
        sb_out = sb_prompt(proj3, p["sb_bias"], p["sb_gain"])
        cos2, sin2 = _rotary_tables(jnp.arange(seq, dtype=jnp.int32))
        ret_out, ret_new = retention(proj3, COL_RQ, cos2, sin2, log_gamma,
                                     jnp.zeros((b, RET_HEADS, HEAD_DIM, HEAD_DIM), F32), HEAD_DIM)
        k_new = proj3[:, :, COL_SK * HEAD_DIM:COL_SV * HEAD_DIM].reshape(b, seq, SB_HEADS, HEAD_DIM)
        v_new = proj3[:, :, COL_SV * HEAD_DIM:COL_RQ * HEAD_DIM].reshape(b, seq, SB_HEADS, HEAD_DIM)
        conv_out = conv_out.reshape(m, CONV_DIM)
        sb_out = sb_out.reshape(m, SB_DIM)
        ret_out = ret_out.reshape(m, RET_DIM)
    else:
        cache_k, cache_v, layer, state_conv, state_ret = cache
        past_len = page_table.shape[1] * PAGE_SIZE
        conv_out, conv_new = conv_step(proj, state_conv, p["conv_w"])
        sk = proj[:, COL_SK * HEAD_DIM:COL_SV * HEAD_DIM].reshape(m, SB_HEADS, HEAD_DIM)
        sv = proj[:, COL_SV * HEAD_DIM:COL_RQ * HEAD_DIM].reshape(m, SB_HEADS, HEAD_DIM)
        sq = proj[:, COL_SQ * HEAD_DIM:COL_SK * HEAD_DIM].reshape(m, SB_HEADS, HEAD_DIM)
        sb_out = sb_decode(sq, cache_k, cache_v, layer, page_table, p["sb_bias"], p["sb_gain"])
        sb_out = sb_out.reshape(m, SB_DIM).astype(BF16)
        k_new, v_new = sk[:, None], sv[:, None]
        cos2, sin2 = _rotary_tables(jnp.full((1,), past_len, jnp.int32))
        pad = HEAD_DIM - 1
        ret3 = jnp.pad(proj[:, None, COL_RQ * HEAD_DIM:], ((0, 0), (0, pad), (0, 0)))
        ret_out, ret_new = retention(ret3, 0, jnp.pad(cos2, ((0, pad), (0, 0))), jnp.pad(sin2, ((0, pad), (0, 0))),
                                     log_gamma, state_ret, 1)
        ret_out = ret_out[:, 0, :]
    x, xn = out_proj(conv_out, sb_out, ret_out, p["w_out"], x, p["g_mix_post"], p["g_ffn_pre"], tm)
    h = gate_up(xn, p["w_gate_up"], tm, 512)
    x, xn = down_proj(h, p["w_down"], x, p["g_ffn_post"], g_next, tm, 512)
    return x, xn, k_new, v_new, conv_new, ret_new


def kernel(x_prompt, x_sample, cache_k, cache_v, state_conv, state_ret, page_table, norm_mix_pre, norm_mix_post, norm_ffn_pre, norm_ffn_post, w_in, conv_w, sb_bias, sb_gain, w_out, w_gate_up, w_down):
    b, seq, _ = x_prompt.shape
    n = x_sample.shape[0]
    depth = w_in.shape[0]
    xp = x_prompt.reshape(b * seq, D_MODEL)
    xs = x_sample.reshape(n, D_MODEL)
    xpn = norm_cast(xp, norm_mix_pre[0], 512)
    xsn = norm_cast(xs, norm_mix_pre[0], n)
    outs_p, outs_s = [], []
    for l in range(depth):
        p = {"w_in": w_in[l].astype(BF16), "w_out": w_out[l].astype(BF16),
             "w_gate_up": w_gate_up[l].astype(BF16), "w_down": w_down[l].astype(BF16),
             "conv_w": conv_w[l], "sb_bias": sb_bias[l], "sb_gain": sb_gain[l],
             "g_mix_post": norm_mix_post[l], "g_ffn_pre": norm_ffn_pre[l], "g_ffn_post": norm_ffn_post[l]}
        g_next = norm_mix_pre[l + 1] if l + 1 < depth else None
        xp, xpn, *rest_p = _layer(("prompt", b, seq), xp, xpn, None, None, p, g_next)
        xs, xsn, *rest_s = _layer(("sample", n), xs, xsn,
                                  (cache_k, cache_v, l, state_conv[l], state_ret[l]), page_table, p, g_next)
        outs_p.append(rest_p)
        outs_s.append(rest_s)
    stack = lambda outs, i: jnp.stack([o[i] for o in outs])
    return (xp.reshape(b, seq, D_MODEL), xs.reshape(n, 1, D_MODEL),
            stack(outs_p, 0), stack(outs_p, 1), stack(outs_p, 2), stack(outs_p, 3),
            stack(outs_s, 0), stack(outs_s, 1), stack(outs_s, 2), stack(outs_s, 3))
```

```python
import functools

import jax
import jax.numpy as jnp
from jax import lax
from jax.experimental import pallas as pl
from jax.experimental.pallas import tpu as pltpu

D_MODEL = 2048
HEAD_DIM = 128
CONV_DIM = 512
SB_DIM = 1024
RET_DIM = 512
SB_HEADS = 8
RET_HEADS = 4
CONV_WIDTH = 3
IN_COLS = 3 * CONV_DIM + 3 * SB_DIM + 4 * RET_DIM
FFN_DIM = 5632
PAGE_SIZE = 128
ROPE_BASE = 10000.0
RMS_EPS = 1e-6
SCALE = HEAD_DIM ** -0.5

COL_CB, COL_CC, COL_CH = 0, 4, 8
COL_SQ, COL_SK, COL_SV = 12, 20, 28
COL_RQ, COL_RK, COL_RV, COL_RG = 36, 40, 44, 48

VMEM_LIMIT_BYTES = 56 * 1024 * 1024
BF16 = jnp.bfloat16
F32 = jnp.float32


def _params(*semantics):
    return pltpu.CompilerParams(dimension_semantics=semantics, vmem_limit_bytes=VMEM_LIMIT_BYTES)


def _rms(x, gain):
    return x * lax.rsqrt(jnp.mean(x * x, axis=-1, keepdims=True) + RMS_EPS) * gain


def _log_stick(z):
    tail = jnp.log(1.0 + jnp.exp(-jnp.abs(z)))
    return jnp.minimum(-z, 0.0) - tail, jnp.minimum(z, 0.0) - tail


def _dot(a, b):
    return jnp.dot(a, b, preferred_element_type=F32)


def _dot_nt(a, b):
    return lax.dot_general(a, b, (((1,), (1,)), ((), ())), preferred_element_type=F32)


def _suffix_matrix():
    r = lax.broadcasted_iota(jnp.int32, (2 * HEAD_DIM, 2 * HEAD_DIM), 0) % HEAD_DIM
    c = lax.broadcasted_iota(jnp.int32, (2 * HEAD_DIM, 2 * HEAD_DIM), 1)
    return jnp.where((r > c) | (c >= HEAD_DIM), 1.0, 0.0).astype(BF16)


def _suffix_and_total(log_rem, tri):
    hi = log_rem.astype(BF16)
    lo = (log_rem - hi.astype(F32)).astype(BF16)
    s = _dot(jnp.concatenate([hi, lo], axis=1), tri)
    return s[:, :HEAD_DIM], s[:, HEAD_DIM:]


def _norm_cast_kernel(x_ref, g_ref, o_ref):
    o_ref[...] = _rms(x_ref[...], g_ref[...]).astype(BF16)


def norm_cast(x, gain, tm):
    m = x.shape[0]
    return pl.pallas_call(
        _norm_cast_kernel,
        grid=(m // tm,),
        in_specs=[pl.BlockSpec((tm, D_MODEL), lambda i: (i, 0)),
                  pl.BlockSpec((1, D_MODEL), lambda i: (0, 0))],
        out_specs=pl.BlockSpec((tm, D_MODEL), lambda i: (i, 0)),
        out_shape=jax.ShapeDtypeStruct((m, D_MODEL), BF16),
        compiler_params=_params("parallel"),
        name="norm_cast",
    )(x, gain.reshape(1, D_MODEL))


def _in_proj_kernel(a_ref, w_ref, o_ref):
    o_ref[...] = _dot(a_ref[...], w_ref[...])


def in_proj(a, w, tm, tn):
    m, k = a.shape
    n = w.shape[1]
    return pl.pallas_call(
        _in_proj_kernel,
        grid=(m // tm, n // tn),
        in_specs=[pl.BlockSpec((tm, k), lambda i, j: (i, 0)),
                  pl.BlockSpec((k, tn), lambda i, j: (0, j))],
        out_specs=pl.BlockSpec((tm, tn), lambda i, j: (i, j)),
        out_shape=jax.ShapeDtypeStruct((m, n), F32),
        compiler_params=_params("parallel", "arbitrary"),
        name="in_proj",
    )(a, w)


def _out_proj_kernel(c_ref, s_ref, r_ref, w_ref, x_ref, gpost_ref, gnext_ref, x_out_ref, xn_out_ref):
    mix = jnp.concatenate([c_ref[...], s_ref[...], r_ref[...]], axis=-1)
    x = x_ref[...] + _rms(_dot(mix, w_ref[...]), gpost_ref[...])
    x_out_ref[...] = x
    xn_out_ref[...] = _rms(x, gnext_ref[...]).astype(BF16)


def out_proj(conv_out, sb_out, ret_out, w, x, g_post, g_next, tm):
    m = x.shape[0]
    row = lambda i: (i, 0)
    fixed = lambda i: (0, 0)
    return pl.pallas_call(
        _out_proj_kernel,
        grid=(m // tm,),
        in_specs=[pl.BlockSpec((tm, CONV_DIM), row),
                  pl.BlockSpec((tm, SB_DIM), row),
                  pl.BlockSpec((tm, RET_DIM), row),
                  pl.BlockSpec((D_MODEL, D_MODEL), fixed),
                  pl.BlockSpec((tm, D_MODEL), row),
                  pl.BlockSpec((1, D_MODEL), fixed),
                  pl.BlockSpec((1, D_MODEL), fixed)],
        out_specs=[pl.BlockSpec((tm, D_MODEL), row), pl.BlockSpec((tm, D_MODEL), row)],
        out_shape=[jax.ShapeDtypeStruct((m, D_MODEL), F32), jax.ShapeDtypeStruct((m, D_MODEL), BF16)],
        compiler_params=_params("parallel"),
        name="out_proj",
    )(conv_out, sb_out, ret_out, w, x, g_post.reshape(1, -1), g_next.reshape(1, -1))


def _gate_up_kernel(a_ref, wg_ref, wu_ref, o_ref):
    a = a_ref[...]
    gate = _dot(a, wg_ref[...])
    up = _dot(a, wu_ref[...])
    o_ref[...] = (gate * jax.nn.sigmoid(gate) * up).astype(BF16)


def gate_up(a, w, tm, tf):
    m = a.shape[0]
    nf = FFN_DIM // tf
    return pl.pallas_call(
        _gate_up_kernel,
        grid=(m // tm, nf),
        in_specs=[pl.BlockSpec((tm, D_MODEL), lambda i, j: (i, 0)),
                  pl.BlockSpec((D_MODEL, tf), lambda i, j: (0, j)),
                  pl.BlockSpec((D_MODEL, tf), lambda i, j: (0, j + nf))],
        out_specs=pl.BlockSpec((tm, tf), lambda i, j: (i, j)),
        out_shape=jax.ShapeDtypeStruct((m, FFN_DIM), BF16),
        compiler_params=_params("parallel", "arbitrary"),
        name="gate_up",
    )(a, w, w)


def _down_kernel(emit_next, h_ref, w_ref, x_ref, gpost_ref, *rest):
    if emit_next:
        gnext_ref, x_out_ref, xn_out_ref, acc_ref = rest
    else:
        x_out_ref, acc_ref = rest
    k = pl.program_id(1)

    @pl.when(k == 0)
    def _():
        acc_ref[...] = jnp.zeros_like(acc_ref)

    acc_ref[...] += _dot(h_ref[...], w_ref[...])

    @pl.when(k == pl.num_programs(1) - 1)
    def _():
        x = x_ref[...] + _rms(acc_ref[...], gpost_ref[...])
        x_out_ref[...] = x
        if emit_next:
            xn_out_ref[...] = _rms(x, gnext_ref[...]).astype(BF16)


def down_proj(h, w, x, g_post, g_next, tm, tk):
    m = x.shape[0]
    emit_next = g_next is not None
    row = lambda i, k: (i, 0)
    fixed = lambda i, k: (0, 0)
    in_specs = [pl.BlockSpec((tm, tk), lambda i, k: (i, k)),
                pl.BlockSpec((tk, D_MODEL), lambda i, k: (k, 0)),
                pl.BlockSpec((tm, D_MODEL), row),
                pl.BlockSpec((1, D_MODEL), fixed)]
    args = [h, w, x, g_post.reshape(1, -1)]
    out_specs = [pl.BlockSpec((tm, D_MODEL), row)]
    out_shape = [jax.ShapeDtypeStruct((m, D_MODEL), F32)]
    if emit_next:
        in_specs.append(pl.BlockSpec((1, D_MODEL), fixed))
        args.append(g_next.reshape(1, -1))
        out_specs.append(pl.BlockSpec((tm, D_MODEL), row))
        out_shape.append(jax.ShapeDtypeStruct((m, D_MODEL), BF16))
    outs = pl.pallas_call(
        functools.partial(_down_kernel, emit_next),
        grid=(m // tm, FFN_DIM // tk),
        in_specs=in_specs,
        out_specs=out_specs,
        out_shape=out_shape,
        scratch_shapes=[pltpu.VMEM((tm, D_MODEL), F32)],
        compiler_params=_params("parallel", "arbitrary"),
        name="down_proj",
    )(*args)
    return (outs[0], outs[1]) if emit_next else (outs[0], None)


def _conv_prompt_kernel(cb_ref, cc_ref, ch_ref, st_ref, w_ref, o_ref, new_ref):
    u = cc_ref[0] * ch_ref[0]
    seq = u.shape[0]
    st = st_ref[0]
    w = w_ref[...]
    row = lax.broadcasted_iota(jnp.int32, u.shape, 0)
    back1 = jnp.where(row == 0, st[1:2], pltpu.roll(u, 1, 0))
    back2 = jnp.where(row == 0, st[0:1], jnp.where(row == 1, st[1:2], pltpu.roll(u, 2, 0)))
    y = back2 * w[0:1] + back1 * w[1:2] + u * w[2:3]
    o_ref[0] = (cb_ref[0] * y).astype(BF16)
    new_ref[0] = u[seq - 2:, :]


def conv_prompt(proj3, state, conv_w):
    b, seq, _ = proj3.shape
    col = lambda off: pl.BlockSpec((1, seq, HEAD_DIM), lambda i, c: (i, 0, off + c))
    return pl.pallas_call(
        _conv_prompt_kernel,
        grid=(b, CONV_DIM // HEAD_DIM),
        in_specs=[col(COL_CB), col(COL_CC), col(COL_CH),
                  pl.BlockSpec((1, CONV_WIDTH - 1, HEAD_DIM), lambda i, c: (i, 0, c)),
                  pl.BlockSpec((CONV_WIDTH, HEAD_DIM), lambda i, c: (0, c))],
        out_specs=[pl.BlockSpec((1, seq, HEAD_DIM), lambda i, c: (i, 0, c)),
                   pl.BlockSpec((1, CONV_WIDTH - 1, HEAD_DIM), lambda i, c: (i, 0, c))],
        out_shape=[jax.ShapeDtypeStruct((b, seq, CONV_DIM), BF16),
                   jax.ShapeDtypeStruct((b, CONV_WIDTH - 1, CONV_DIM), F32)],
        compiler_params=_params("parallel", "parallel"),
        name="conv_prompt",
    )(proj3, proj3, proj3, state, conv_w)


def _conv_step_kernel(cb_ref, cc_ref, ch_ref, s0_ref, s1_ref, w_ref, o_ref, u_ref):
    u = cc_ref[...] * ch_ref[...]
    w = w_ref[...]
    y = s0_ref[...] * w[0:1] + s1_ref[...] * w[1:2] + u * w[2:3]
    o_ref[...] = (cb_ref[...] * y).astype(BF16)
    u_ref[...] = u


def conv_step(proj, state, conv_w):
    n = proj.shape[0]
    col = lambda j: pl.BlockSpec((n, CONV_DIM), lambda i: (0, j))
    full = pl.BlockSpec((n, CONV_DIM), lambda i: (0, 0))
    out, u = pl.pallas_call(
        _conv_step_kernel,
        grid=(1,),
        in_specs=[col(0), col(1), col(2), full, full,
                  pl.BlockSpec((CONV_WIDTH, CONV_DIM), lambda i: (0, 0))],
        out_specs=[full, full],
        out_shape=[jax.ShapeDtypeStruct((n, CONV_DIM), BF16), jax.ShapeDtypeStruct((n, CONV_DIM), F32)],
        compiler_params=_params("arbitrary"),
        name="conv_step",
    )(proj, proj, proj, state[:, 0], state[:, 1], conv_w)
    return out, jnp.stack([state[:, 1], u], axis=1)


def _sb_prompt_kernel(q_ref, k_ref, v_ref, bias_ref, gain_ref, o_ref):
    qi = pl.program_id(2)
    tq = q_ref.shape[1]
    q = q_ref[0].astype(BF16)
    bias = bias_ref[0]
    tri = _suffix_matrix()
    n_blk = tq // HEAD_DIM
    bias = jnp.concatenate([bias] * n_blk, axis=1)

    def key_group(g, carry, masked):
        acc, later = carry
        start = pl.multiple_of(g * tq, tq)
        kg = k_ref[0, pl.ds(start, tq), :].astype(BF16)
        vg = v_ref[0, pl.ds(start, tq), :].astype(BF16)
        log_rem, log_beta = _log_stick(_dot_nt(q, kg) * SCALE + bias)
        if masked:
            mask = (lax.broadcasted_iota(jnp.int32, (tq, tq), 1)
                    < lax.broadcasted_iota(jnp.int32, (tq, tq), 0))
            log_rem = jnp.where(mask, log_rem, 0.0)
        blocks = [log_rem[:, j * HEAD_DIM:(j + 1) * HEAD_DIM] for j in range(n_blk)]
        suffix, total = _suffix_and_total(jnp.concatenate(blocks, axis=0), tri)
        parts = [None] * n_blk
        for j in reversed(range(n_blk)):
            parts[j] = suffix[j * tq:(j + 1) * tq] + later
            later = later + total[j * tq:(j + 1) * tq]
        a = jnp.exp(log_beta + jnp.concatenate(parts, axis=1))
        if masked:
            a = jnp.where(mask, a, 0.0)
        return acc + _dot(a.astype(BF16), vg), later

    zero = jnp.zeros((tq, HEAD_DIM), F32)
    carry = key_group(qi, (zero, zero), True)
    carry = lax.fori_loop(0, qi, lambda t, c: key_group(qi - 1 - t, c, False), carry)
    o_ref[0] = (_rms(carry[0], 1.0) * gain_ref[0]).astype(BF16)


def sb_prompt(proj3, bias, gain):
    b, seq, _ = proj3.shape
    tq = 4 * HEAD_DIM
    kv = lambda off: pl.BlockSpec((1, seq, HEAD_DIM), lambda i, h, q: (i, 0, off + h))
    per_head = pl.BlockSpec((1, 1, HEAD_DIM), lambda i, h, q: (h, 0, 0))
    return pl.pallas_call(
        _sb_prompt_kernel,
        grid=(b, SB_HEADS, seq // tq),
        in_specs=[pl.BlockSpec((1, tq, HEAD_DIM), lambda i, h, q: (i, q, COL_SQ + h)),
                  kv(COL_SK), kv(COL_SV), per_head, per_head],
        out_specs=pl.BlockSpec((1, tq, HEAD_DIM), lambda i, h, q: (i, q, h)),
        out_shape=jax.ShapeDtypeStruct((b, seq, SB_DIM), BF16),
        compiler_params=_params("parallel", "parallel", "arbitrary"),
        name="sb_prompt",
    )(proj3, proj3, proj3,
      jnp.broadcast_to(bias[:, None, None], (SB_HEADS, 1, HEAD_DIM)),
      gain.reshape(SB_HEADS, 1, HEAD_DIM))


DECODE_PAGES_PER_STEP = 8


def _sb_decode_kernel(pt_ref, q_ref, *refs):
    del pt_ref
    n_pg = DECODE_PAGES_PER_STEP
    k_refs, v_refs = refs[:n_pg], refs[n_pg:2 * n_pg]
    bias_ref, gain_ref, o_ref, acc_ref, later_ref = refs[2 * n_pg:]
    t = pl.program_id(1)

    @pl.when(t == 0)
    def _():
        acc_ref[...] = jnp.zeros_like(acc_ref)
        later_ref[...] = jnp.zeros_like(later_ref)

    q = q_ref[0]
    head = lax.broadcasted_iota(jnp.int32, (SB_HEADS, HEAD_DIM), 0)
    only = lambda x, h: jnp.where(head == h, x, 0.0).astype(BF16)
    q_rows = [only(q, h) for h in range(SB_HEADS)]
    head_rows = lambda ref, h: ref[0, 0, pl.ds(h, PAGE_SIZE, stride=SB_HEADS), :].astype(BF16)
    zs = []
    for k_ref in k_refs:
        z = _dot_nt(q_rows[0], head_rows(k_ref, 0))
        for h in range(1, SB_HEADS):
            z = z + _dot_nt(q_rows[h], head_rows(k_ref, h))
        zs.append(z)
    bias = jnp.concatenate([bias_ref[...]] * n_pg, axis=0)
    log_rem, log_beta = _log_stick(jnp.concatenate(zs, axis=0) * SCALE + bias)
    suffix, total = _suffix_and_total(log_rem, _suffix_matrix())
    later = later_ref[...]
    parts = []
    for i in range(n_pg):
        parts.append(suffix[i * SB_HEADS:(i + 1) * SB_HEADS] + later)
        later = later + total[i * SB_HEADS:(i + 1) * SB_HEADS]
    later_ref[...] = later
    a = jnp.exp(log_beta + jnp.concatenate(parts, axis=0))
    acc = acc_ref[...]
    for i, v_ref in enumerate(v_refs):
        a_page = a[i * SB_HEADS:(i + 1) * SB_HEADS]
        out = _dot(only(a_page, 0), head_rows(v_ref, 0))
        for h in range(1, SB_HEADS):
            out = out + _dot(only(a_page, h), head_rows(v_ref, h))
        acc = acc + out
    acc_ref[...] = acc

    @pl.when(t == pl.num_programs(1) - 1)
    def _():
        o_ref[0] = _rms(acc, 1.0) * gain_ref[...]


def sb_decode(q, cache_k, cache_v, layer, page_table, bias, gain):
    n, n_pages = page_table.shape
    n_pg = DECODE_PAGES_PER_STEP
    assert n_pages % n_pg == 0
    rows = PAGE_SIZE * SB_HEADS
    depth, n_pool = cache_k.shape[:2]
    ck = cache_k.reshape(depth, n_pool, rows, HEAD_DIM)
    cv = cache_v.reshape(depth, n_pool, rows, HEAD_DIM)
    page = lambda i_pg: pl.BlockSpec(
        (1, 1, rows, HEAD_DIM), lambda i, t, pt: (layer, pt[i, n_pages - 1 - (n_pg * t + i_pg)], 0, 0))
    pages = [page(i_pg) for i_pg in range(n_pg)]
    per_head = pl.BlockSpec((SB_HEADS, HEAD_DIM), lambda i, t, pt: (0, 0))
    seq_block = pl.BlockSpec((1, SB_HEADS, HEAD_DIM), lambda i, t, pt: (i, 0, 0))
    return pl.pallas_call(
        _sb_decode_kernel,
        grid_spec=pltpu.PrefetchScalarGridSpec(
            num_scalar_prefetch=1,
            grid=(n, n_pages // n_pg),
            in_specs=[seq_block] + pages + pages + [per_head, per_head],
            out_specs=seq_block,
            scratch_shapes=[pltpu.VMEM((SB_HEADS, HEAD_DIM), F32), pltpu.VMEM((SB_HEADS, HEAD_DIM), F32)]),
        out_shape=jax.ShapeDtypeStruct((n, SB_HEADS, HEAD_DIM), F32),
        compiler_params=_params("parallel", "arbitrary"),
        name="sb_decode",
    )(page_table, q, *([ck] * n_pg), *([cv] * n_pg),
      jnp.broadcast_to(bias[:, None], (SB_HEADS, HEAD_DIM)),
      gain.reshape(SB_HEADS, HEAD_DIM))


def _rotate(x, cos2, sin2):
    return x * cos2 + pltpu.roll(x, HEAD_DIM // 2, 1) * sin2


def _retention_kernel(chunk, n_chunks, q_ref, k_ref, v_ref, g_ref, cos_ref, sin_ref, lg_ref, s_ref,
                      o_ref, s_out_ref):
    lg = lg_ref[0]
    r = lax.broadcasted_iota(jnp.int32, (HEAD_DIM, HEAD_DIM), 0).astype(F32)
    c = lax.broadcasted_iota(jnp.int32, (HEAD_DIM, HEAD_DIM), 1).astype(F32)
    diff = r - c
    dmat = jnp.where(diff >= 0, jnp.exp(jnp.maximum(diff, 0.0) * lg), 0.0)
    cross = jnp.exp((r + 1.0) * lg)
    k_decay = jnp.exp((chunk - 1.0 - r) * lg)
    s_decay = jnp.exp(chunk * lg)

    def body(i, s):
        start = pl.multiple_of(i * HEAD_DIM, HEAD_DIM)
        rows = pl.ds(start, HEAD_DIM)
        cos2 = cos_ref[rows, :]
        sin2 = sin_ref[rows, :]
        q = _rotate(q_ref[0, rows, :], cos2, sin2).astype(BF16)
        k = _rotate(k_ref[0, rows, :], cos2, sin2) * SCALE
        v = v_ref[0, rows, :].astype(BF16)
        inner = _dot_nt(q, k.astype(BF16)) * dmat
        o = _dot(inner.astype(BF16), v) + _dot(q, s.astype(BF16)) * cross
        g = g_ref[0, rows, :]
        o_ref[0, rows, :] = (g * jax.nn.sigmoid(g) * _rms(o, 1.0)).astype(o_ref.dtype)
        return s_decay * s + _dot((k * k_decay).T.astype(BF16), v)

    s_out_ref[0, 0] = lax.fori_loop(0, n_chunks, body, s_ref[0, 0])


def retention(proj3, col_q, cos2, sin2, log_gamma, state, chunk):
    b, rows, _ = proj3.shape
    n_chunks = rows // HEAD_DIM
    col = lambda off: pl.BlockSpec((1, rows, HEAD_DIM), lambda i, h: (i, 0, col_q + off + h))
    table = pl.BlockSpec((rows, HEAD_DIM), lambda i, h: (0, 0))
    st = pl.BlockSpec((1, 1, HEAD_DIM, HEAD_DIM), lambda i, h: (i, h, 0, 0))
    return pl.pallas_call(
        functools.partial(_retention_kernel, float(chunk), n_chunks),
        grid=(b, RET_HEADS),
        in_specs=[col(0), col(RET_HEADS), col(2 * RET_HEADS), col(3 * RET_HEADS), table, table,
                  pl.BlockSpec((1, 1, HEAD_DIM), lambda i, h: (h, 0, 0)), st],
        out_specs=[pl.BlockSpec((1, rows, HEAD_DIM), lambda i, h: (i, 0, h)), st],
        out_shape=[jax.ShapeDtypeStruct((b, rows, RET_DIM), BF16),
                   jax.ShapeDtypeStruct((b, RET_HEADS, HEAD_DIM, HEAD_DIM), F32)],
        compiler_params=_params("parallel", "parallel"),
        name="retention",
    )(proj3, proj3, proj3, proj3, cos2, sin2,
      jnp.broadcast_to(log_gamma[:, None, None], (RET_HEADS, 1, HEAD_DIM)), state)


def _rotary_tables(pos):
    inv = ROPE_BASE ** (-jnp.arange(0, HEAD_DIM, 2, dtype=F32) / HEAD_DIM)
    ang = pos.astype(F32)[:, None] * inv[None, :]
    cos, sin = jnp.cos(ang), jnp.sin(ang)
    return jnp.concatenate([cos, cos], axis=-1), jnp.concatenate([-sin, sin], axis=-1)


def _layer(group, x, xn, cache, page_table, p, g_next):
    is_prompt = group[0] == "prompt"
    m = x.shape[0]
    tm = 512 if is_prompt else m
    proj = in_proj(xn, p["w_in"], tm, 1664)
    log_gamma = jnp.log1p(-jnp.exp2(-5.0 - jnp.arange(RET_HEADS, dtype=F32)))
    if is_prompt:
        _, b, seq = group
        proj3 = proj.reshape(b, seq, IN_COLS)
        conv_out, conv_new = conv_prompt(proj3, jnp.zeros((b, CONV_WIDTH - 1, CONV_DIM), F32), p["conv_w"])
        sb_out = sb_prompt(proj3, p["sb_bias"], p["sb_gain"])
        cos2, sin2 = _rotary_tables(jnp.arange(seq, dtype=jnp.int32))
        ret_out, ret_new = retention(proj3, COL_RQ, cos2, sin2, log_gamma,
                                     jnp.zeros((b, RET_HEADS, HEAD_DIM, HEAD_DIM), F32), HEAD_DIM)
        k_new = proj3[:, :, COL_SK * HEAD_DIM:COL_SV * HEAD_DIM].reshape(b, seq, SB_HEADS, HEAD_DIM)
        v_new = proj3[:, :, COL_SV * HEAD_DIM:COL_RQ * HEAD_DIM].reshape(b, seq, SB_HEADS, HEAD_DIM)
        conv_out = conv_out.reshape(m, CONV_DIM)
        sb_out = sb_out.reshape(m, SB_DIM)
        ret_out = ret_out.reshape(m, RET_DIM)
    else:
        cache_k, cache_v, layer, state_conv, state_ret = cache
        past_len = page_table.shape[1] * PAGE_SIZE
        conv_out, conv_new = conv_step(proj, state_conv, p["conv_w"])
        sk = proj[:, COL_SK * HEAD_DIM:COL_SV * HEAD_DIM].reshape(m, SB_HEADS, HEAD_DIM)
        sv = proj[:, COL_SV * HEAD_DIM:COL_RQ * HEAD_DIM].reshape(m, SB_HEADS, HEAD_DIM)
        sq = proj[:, COL_SQ * HEAD_DIM:COL_SK * HEAD_DIM].reshape(m, SB_HEADS, HEAD_DIM)
        sb_out = sb_decode(sq, cache_k, cache_v, layer, page_table, p["sb_bias"], p["sb_gain"])
        sb_out = sb_out.reshape(m, SB_DIM).astype(BF16)
        k_new, v_new = sk[:, None], sv[:, None]
        cos2, sin2 = _rotary_tables(jnp.full((1,), past_len, jnp.int32))
        pad = HEAD_DIM - 1
        ret3 = jnp.pad(proj[:, None, COL_RQ * HEAD_DIM:], ((0, 0), (0, pad), (0, 0)))
        ret_out, ret_new = retention(ret3, 0, jnp.pad(cos2, ((0, pad), (0, 0))), jnp.pad(sin2, ((0, pad), (0, 0))),
                                     log_gamma, state_ret, 1)
        ret_out = ret_out[:, 0, :]
    x, xn = out_proj(conv_out, sb_out, ret_out, p["w_out"], x, p["g_mix_post"], p["g_ffn_pre"], tm)
    h = gate_up(xn, p["w_gate_up"], tm, 512)
    x, xn = down_proj(h, p["w_down"], x, p["g_ffn_post"], g_next, tm, FFN_DIM // 4)
    return x, xn, k_new, v_new, conv_new, ret_new


def kernel(x_prompt, x_sample, cache_k, cache_v, state_conv, state_ret, page_table, norm_mix_pre, norm_mix_post, norm_ffn_pre, norm_ffn_post, w_in, conv_w, sb_bias, sb_gain, w_out, w_gate_up, w_down):
    b, seq, _ = x_prompt.shape
    n = x_sample.shape[0]
    depth = w_in.shape[0]
    xp = x_prompt.reshape(b * seq, D_MODEL)
    xs = x_sample.reshape(n, D_MODEL)
    xpn = norm_cast(xp, norm_mix_pre[0], 512)
    xsn = norm_cast(xs, norm_mix_pre[0], n)
    outs_p, outs_s = [], []
    for l in range(depth):
        p = {"w_in": w_in[l].astype(BF16), "w_out": w_out[l].astype(BF16),
             "w_gate_up": w_gate_up[l].astype(BF16), "w_down": w_down[l].astype(BF16),
             "conv_w": conv_w[l], "sb_bias": sb_bias[l], "sb_gain": sb_gain[l],
             "g_mix_post": norm_mix_post[l], "g_ffn_pre": norm_ffn_pre[l], "g_ffn_post": norm_ffn_post[l]}
        g_next = norm_mix_pre[l + 1] if l + 1 < depth else None
        xp, xpn, *rest_p = _layer(("prompt", b, seq), xp, xpn, None, None, p, g_next)
        xs, xsn, *rest_s = _layer(("sample", n), xs, xsn,
                                  (cache_k, cache_v, l, state_conv[l], state_ret[l]), page_table, p, g_next)
        outs_p.append(rest_p)
        outs_s.append(rest_s)
    stack = lambda outs, i: jnp.stack([o[i] for o in outs])
    return (xp.reshape(b, seq, D_MODEL), xs.reshape(n, 1, D_MODEL),
            stack(outs_p, 0), stack(outs_p, 1), stack(outs_p, 2), stack(outs_p, 3),
            stack(outs_s, 0), stack(outs_s, 1), stack(outs_s, 2), stack(outs_s, 3))
```

```python
import functools

import jax
import jax.numpy as jnp
from jax import lax
from jax.experimental import pallas as pl
from jax.experimental.pallas import tpu as pltpu

D_MODEL = 2048
HEAD_DIM = 128
CONV_DIM = 512
SB_DIM = 1024
RET_DIM = 512
SB_HEADS = 8
RET_HEADS = 4
CONV_WIDTH = 3
IN_COLS = 3 * CONV_DIM + 3 * SB_DIM + 4 * RET_DIM
FFN_DIM = 5632
PAGE_SIZE = 128
ROPE_BASE = 10000.0
RMS_EPS = 1e-6
SCALE = HEAD_DIM ** -0.5

COL_CB, COL_CC, COL_CH = 0, 4, 8
COL_SQ, COL_SK, COL_SV = 12, 20, 28
COL_RQ, COL_RK, COL_RV, COL_RG = 36, 40, 44, 48

VMEM_LIMIT_BYTES = 56 * 1024 * 1024
BF16 = jnp.bfloat16
F32 = jnp.float32


def _params(*semantics):
    return pltpu.CompilerParams(dimension_semantics=semantics, vmem_limit_bytes=VMEM_LIMIT_BYTES)


def _rms(x, gain):
    return x * lax.rsqrt(jnp.mean(x * x, axis=-1, keepdims=True) + RMS_EPS) * gain


def _log_stick(z):
    tail = jnp.log(1.0 + jnp.exp(-jnp.abs(z)))
    return jnp.minimum(-z, 0.0) - tail, jnp.minimum(z, 0.0) - tail


def _dot(a, b):
    return jnp.dot(a, b, preferred_element_type=F32)


def _dot_nt(a, b):
    return lax.dot_general(a, b, (((1,), (1,)), ((), ())), preferred_element_type=F32)


def _suffix_matrix():
    r = lax.broadcasted_iota(jnp.int32, (2 * HEAD_DIM, 2 * HEAD_DIM), 0) % HEAD_DIM
    c = lax.broadcasted_iota(jnp.int32, (2 * HEAD_DIM, 2 * HEAD_DIM), 1)
    return jnp.where((r > c) | (c >= HEAD_DIM), 1.0, 0.0).astype(BF16)


def _suffix_and_total(log_rem, tri):
    hi = log_rem.astype(BF16)
    lo = (log_rem - hi.astype(F32)).astype(BF16)
    s = _dot(jnp.concatenate([hi, lo], axis=1), tri)
    return s[:, :HEAD_DIM], s[:, HEAD_DIM:]


def _norm_cast_kernel(x_ref, g_ref, o_ref):
    o_ref[...] = _rms(x_ref[...], g_ref[...]).astype(BF16)


def norm_cast(x, gain, tm):
    m = x.shape[0]
    return pl.pallas_call(
        _norm_cast_kernel,
        grid=(m // tm,),
        in_specs=[pl.BlockSpec((tm, D_MODEL), lambda i: (i, 0)),
                  pl.BlockSpec((1, D_MODEL), lambda i: (0, 0))],
        out_specs=pl.BlockSpec((tm, D_MODEL), lambda i: (i, 0)),
        out_shape=jax.ShapeDtypeStruct((m, D_MODEL), BF16),
        compiler_params=_params("parallel"),
        name="norm_cast",
    )(x, gain.reshape(1, D_MODEL))


def _in_proj_kernel(a_ref, w_ref, o_ref):
    o_ref[...] = _dot(a_ref[...], w_ref[...])


def in_proj(a, w, tm, tn):
    m, k = a.shape
    n = w.shape[1]
    return pl.pallas_call(
        _in_proj_kernel,
        grid=(m // tm, n // tn),
        in_specs=[pl.BlockSpec((tm, k), lambda i, j: (i, 0)),
                  pl.BlockSpec((k, tn), lambda i, j: (0, j))],
        out_specs=pl.BlockSpec((tm, tn), lambda i, j: (i, j)),
        out_shape=jax.ShapeDtypeStruct((m, n), F32),
        compiler_params=_params("parallel", "arbitrary"),
        name="in_proj",
    )(a, w)


def _out_proj_kernel(c_ref, s_ref, r_ref, w_ref, x_ref, gpost_ref, gnext_ref, x_out_ref, xn_out_ref):
    mix = jnp.concatenate([c_ref[...], s_ref[...], r_ref[...]], axis=-1)
    x = x_ref[...] + _rms(_dot(mix, w_ref[...]), gpost_ref[...])
    x_out_ref[...] = x
    xn_out_ref[...] = _rms(x, gnext_ref[...]).astype(BF16)


def out_proj(conv_out, sb_out, ret_out, w, x, g_post, g_next, tm):
    m = x.shape[0]
    row = lambda i: (i, 0)
    fixed = lambda i: (0, 0)
    return pl.pallas_call(
        _out_proj_kernel,
        grid=(m // tm,),
        in_specs=[pl.BlockSpec((tm, CONV_DIM), row),
                  pl.BlockSpec((tm, SB_DIM), row),
                  pl.BlockSpec((tm, RET_DIM), row),
                  pl.BlockSpec((D_MODEL, D_MODEL), fixed),
                  pl.BlockSpec((tm, D_MODEL), row),
                  pl.BlockSpec((1, D_MODEL), fixed),
                  pl.BlockSpec((1, D_MODEL), fixed)],
        out_specs=[pl.BlockSpec((tm, D_MODEL), row), pl.BlockSpec((tm, D_MODEL), row)],
        out_shape=[jax.ShapeDtypeStruct((m, D_MODEL), F32), jax.ShapeDtypeStruct((m, D_MODEL), BF16)],
        compiler_params=_params("parallel"),
        name="out_proj",
    )(conv_out, sb_out, ret_out, w, x, g_post.reshape(1, -1), g_next.reshape(1, -1))


def _gate_up_kernel(a_ref, wg_ref, wu_ref, o_ref):
    a = a_ref[...]
    gate = _dot(a, wg_ref[...])
    up = _dot(a, wu_ref[...])
    o_ref[...] = (gate * jax.nn.sigmoid(gate) * up).astype(BF16)


def gate_up(a, w, tm, tf):
    m = a.shape[0]
    nf = FFN_DIM // tf
    return pl.pallas_call(
        _gate_up_kernel,
        grid=(m // tm, nf),
        in_specs=[pl.BlockSpec((tm, D_MODEL), lambda i, j: (i, 0)),
                  pl.BlockSpec((D_MODEL, tf), lambda i, j: (0, j)),
                  pl.BlockSpec((D_MODEL, tf), lambda i, j: (0, j + nf))],
        out_specs=pl.BlockSpec((tm, tf), lambda i, j: (i, j)),
        out_shape=jax.ShapeDtypeStruct((m, FFN_DIM), BF16),
        compiler_params=_params("parallel", "arbitrary"),
        name="gate_up",
    )(a, w, w)


def _down_kernel(emit_next, h_ref, w_ref, x_ref, gpost_ref, *rest):
    if emit_next:
        gnext_ref, x_out_ref, xn_out_ref, acc_ref = rest
    else:
        x_out_ref, acc_ref = rest
    k = pl.program_id(1)

    @pl.when(k == 0)
    def _():
        acc_ref[...] = jnp.zeros_like(acc_ref)

    acc_ref[...] += _dot(h_ref[...], w_ref[...])

    @pl.when(k == pl.num_programs(1) - 1)
    def _():
        x = x_ref[...] + _rms(acc_ref[...], gpost_ref[...])
        x_out_ref[...] = x
        if emit_next:
            xn_out_ref[...] = _rms(x, gnext_ref[...]).astype(BF16)


def down_proj(h, w, x, g_post, g_next, tm, tk):
    m = x.shape[0]
    emit_next = g_next is not None
    row = lambda i, k: (i, 0)
    fixed = lambda i, k: (0, 0)
    in_specs = [pl.BlockSpec((tm, tk), lambda i, k: (i, k)),
                pl.BlockSpec((tk, D_MODEL), lambda i, k: (k, 0)),
                pl.BlockSpec((tm, D_MODEL), row),
                pl.BlockSpec((1, D_MODEL), fixed)]
    args = [h, w, x, g_post.reshape(1, -1)]
    out_specs = [pl.BlockSpec((tm, D_MODEL), row)]
    out_shape = [jax.ShapeDtypeStruct((m, D_MODEL), F32)]
    if emit_next:
        in_specs.append(pl.BlockSpec((1, D_MODEL), fixed))
        args.append(g_next.reshape(1, -1))
        out_specs.append(pl.BlockSpec((tm, D_MODEL), row))
        out_shape.append(jax.ShapeDtypeStruct((m, D_MODEL), BF16))
    outs = pl.pallas_call(
        functools.partial(_down_kernel, emit_next),
        grid=(m // tm, FFN_DIM // tk),
        in_specs=in_specs,
        out_specs=out_specs,
        out_shape=out_shape,
        scratch_shapes=[pltpu.VMEM((tm, D_MODEL), F32)],
        compiler_params=_params("parallel", "arbitrary"),
        name="down_proj",
    )(*args)
    return (outs[0], outs[1]) if emit_next else (outs[0], None)


def _conv_prompt_kernel(cb_ref, cc_ref, ch_ref, st_ref, w_ref, o_ref, new_ref):
    u = cc_ref[0] * ch_ref[0]
    seq = u.shape[0]
    st = st_ref[0]
    w = w_ref[...]
    row = lax.broadcasted_iota(jnp.int32, u.shape, 0)
    back1 = jnp.where(row == 0, st[1:2], pltpu.roll(u, 1, 0))
    back2 = jnp.where(row == 0, st[0:1], jnp.where(row == 1, st[1:2], pltpu.roll(u, 2, 0)))
    y = back2 * w[0:1] + back1 * w[1:2] + u * w[2:3]
    o_ref[0] = (cb_ref[0] * y).astype(BF16)
    new_ref[0] = u[seq - 2:, :]


def conv_prompt(proj3, state, conv_w):
    b, seq, _ = proj3.shape
    col = lambda off: pl.BlockSpec((1, seq, HEAD_DIM), lambda i, c: (i, 0, off + c))
    return pl.pallas_call(
        _conv_prompt_kernel,
        grid=(b, CONV_DIM // HEAD_DIM),
        in_specs=[col(COL_CB), col(COL_CC), col(COL_CH),
                  pl.BlockSpec((1, CONV_WIDTH - 1, HEAD_DIM), lambda i, c: (i, 0, c)),
                  pl.BlockSpec((CONV_WIDTH, HEAD_DIM), lambda i, c: (0, c))],
        out_specs=[pl.BlockSpec((1, seq, HEAD_DIM), lambda i, c: (i, 0, c)),
                   pl.BlockSpec((1, CONV_WIDTH - 1, HEAD_DIM), lambda i, c: (i, 0, c))],
        out_shape=[jax.ShapeDtypeStruct((b, seq, CONV_DIM), BF16),
                   jax.ShapeDtypeStruct((b, CONV_WIDTH - 1, CONV_DIM), F32)],
        compiler_params=_params("parallel", "parallel"),
        name="conv_prompt",
    )(proj3, proj3, proj3, state, conv_w)


def _conv_step_kernel(cb_ref, cc_ref, ch_ref, s0_ref, s1_ref, w_ref, o_ref, u_ref):
    u = cc_ref[...] * ch_ref[...]
    w = w_ref[...]
    y = s0_ref[...] * w[0:1] + s1_ref[...] * w[1:2] + u * w[2:3]
    o_ref[...] = (cb_ref[...] * y).astype(BF16)
    u_ref[...] = u


def conv_step(proj, state, conv_w):
    n = proj.shape[0]
    col = lambda j: pl.BlockSpec((n, CONV_DIM), lambda i: (0, j))
    full = pl.BlockSpec((n, CONV_DIM), lambda i: (0, 0))
    out, u = pl.pallas_call(
        _conv_step_kernel,
        grid=(1,),
        in_specs=[col(0), col(1), col(2), full, full,
                  pl.BlockSpec((CONV_WIDTH, CONV_DIM), lambda i: (0, 0))],
        out_specs=[full, full],
        out_shape=[jax.ShapeDtypeStruct((n, CONV_DIM), BF16), jax.ShapeDtypeStruct((n, CONV_DIM), F32)],
        compiler_params=_params("arbitrary"),
        name="conv_step",
    )(proj, proj, proj, state[:, 0], state[:, 1], conv_w)
    return out, jnp.stack([state[:, 1], u], axis=1)


def _sb_prompt_kernel(q_ref, k_ref, v_ref, bias_ref, gain_ref, *rest):
    o_ref, k_out_ref, v_out_ref = rest[-3:]
    qi = pl.program_id(2)
    tq = q_ref.shape[1]

    @pl.when(qi == 0)
    def _():
        k_out_ref[...] = k_ref[...]
        v_out_ref[...] = v_ref[...]

    q = q_ref[0].astype(BF16)
    bias = bias_ref[0]
    tri = _suffix_matrix()
    n_blk = tq // HEAD_DIM
    bias = jnp.concatenate([bias] * n_blk, axis=1)

    n_str = 2
    rows = tq // n_str
    qs = [q[i * rows:(i + 1) * rows] for i in range(n_str)]

    def key_group(g, carry, masked):
        accs, laters = carry
        start = pl.multiple_of(g * tq, tq)
        kg = k_ref[0, pl.ds(start, tq), :].astype(BF16)
        vg = v_ref[0, pl.ds(start, tq), :].astype(BF16)
        zs = [_dot_nt(qs[i], kg) for i in range(n_str)]
        stacks, betas, masks = [], [], []
        for i in range(n_str):
            log_rem, log_beta = _log_stick(zs[i] * SCALE + bias)
            if masked:
                mask = (lax.broadcasted_iota(jnp.int32, (rows, tq), 1)
                        < lax.broadcasted_iota(jnp.int32, (rows, tq), 0) + i * rows)
                log_rem = jnp.where(mask, log_rem, 0.0)
                masks.append(mask)
            blocks = [log_rem[:, j * HEAD_DIM:(j + 1) * HEAD_DIM] for j in range(n_blk)]
            stacks.append(jnp.concatenate(blocks, axis=0))
            betas.append(log_beta)
        sums = [_suffix_and_total(stacks[i], tri) for i in range(n_str)]
        new_accs, new_laters = [], []
        weights = []
        for i in range(n_str):
            suffix, total = sums[i]
            later = laters[i]
            parts = [None] * n_blk
            for j in reversed(range(n_blk)):
                parts[j] = suffix[j * rows:(j + 1) * rows] + later
                later = later + total[j * rows:(j + 1) * rows]
            a = jnp.exp(betas[i] + jnp.concatenate(parts, axis=1))
            if masked:
                a = jnp.where(masks[i], a, 0.0)
            weights.append(a.astype(BF16))
            new_laters.append(later)
        for i in range(n_str):
            new_accs.append(accs[i] + _dot(weights[i], vg))
        return tuple(new_accs), tuple(new_laters)

    zero = tuple(jnp.zeros((rows, HEAD_DIM), F32) for _ in range(n_str))
    carry = key_group(qi, (zero, zero), True)
    carry = lax.fori_loop(0, qi, lambda t, c: key_group(qi - 1 - t, c, False), carry)
    o_ref[0] = (_rms(jnp.concatenate(carry[0], axis=0), 1.0) * gain_ref[0]).astype(BF16)


def sb_prompt(proj3, bias, gain, layer, depth, kv_all):
    b, seq, _ = proj3.shape
    tq = 4 * HEAD_DIM
    kv = lambda off: pl.BlockSpec((1, seq, HEAD_DIM), lambda i, h, q: (i, 0, off + h))
    per_head = pl.BlockSpec((1, 1, HEAD_DIM), lambda i, h, q: (h, 0, 0))
    kv_out = pl.BlockSpec((None, 1, seq, HEAD_DIM), lambda i, h, q: (layer, i, 0, h))
    kv_shape = jax.ShapeDtypeStruct((depth, b, seq, SB_DIM), F32)
    in_specs = [pl.BlockSpec((1, tq, HEAD_DIM), lambda i, h, q: (i, q, COL_SQ + h)),
                kv(COL_SK), kv(COL_SV), per_head, per_head]
    args = [proj3, proj3, proj3,
            jnp.broadcast_to(bias[:, None, None], (SB_HEADS, 1, HEAD_DIM)),
            gain.reshape(SB_HEADS, 1, HEAD_DIM)]
    aliases = {}
    if kv_all is not None:
        aliases = {len(args): 1, len(args) + 1: 2}
        in_specs += [pl.BlockSpec(memory_space=pl.ANY)] * 2
        args += list(kv_all)
    sb_out, k_all, v_all = pl.pallas_call(
        _sb_prompt_kernel,
        grid=(b, SB_HEADS, seq // tq),
        in_specs=in_specs,
        out_specs=[pl.BlockSpec((1, tq, HEAD_DIM), lambda i, h, q: (i, q, h)), kv_out, kv_out],
        out_shape=[jax.ShapeDtypeStruct((b, seq, SB_DIM), BF16), kv_shape, kv_shape],
        input_output_aliases=aliases,
        compiler_params=_params("parallel", "parallel", "arbitrary"),
        name="sb_prompt",
    )(*args)
    return sb_out, (k_all, v_all)


DECODE_PAGES_PER_STEP = 8


def _sb_decode_kernel(n_groups, pt_ref, q_ref, *refs):
    del pt_ref
    n_pg = DECODE_PAGES_PER_STEP
    k_refs, v_refs = refs[:n_pg], refs[n_pg:2 * n_pg]
    bias_ref, gain_ref, o_ref, acc_ref, later_ref, a_ref = refs[2 * n_pg:]
    s = pl.program_id(0)
    last = pl.num_programs(0) - 1
    group_k = jnp.minimum(s, last - 1) % n_groups
    group_v = jnp.maximum(s - 1, 0) % n_groups

    @pl.when(s == 0)
    def _():
        acc_ref[...] = jnp.zeros_like(acc_ref)
        later_ref[...] = jnp.zeros_like(later_ref)
        a_ref[...] = jnp.zeros_like(a_ref)

    head = lax.broadcasted_iota(jnp.int32, (SB_HEADS, HEAD_DIM), 0)
    only = lambda x, h: jnp.where(head == h, x, 0.0).astype(BF16)
    head_rows = lambda ref, h: ref[0, 0, pl.ds(h, PAGE_SIZE, stride=SB_HEADS), :].astype(BF16)

    a_prev = a_ref[...]
    acc = jnp.where(group_v == 0, 0.0, acc_ref[...])
    for i, v_ref in enumerate(v_refs):
        a_page = a_prev[i * SB_HEADS:(i + 1) * SB_HEADS]
        out = _dot(only(a_page, 0), head_rows(v_ref, 0))
        for h in range(1, SB_HEADS):
            out = out + _dot(only(a_page, h), head_rows(v_ref, h))
        acc = acc + out
    acc_ref[...] = acc

    q = q_ref[0]
    q_rows = [only(q, h) for h in range(SB_HEADS)]
    zs = []
    for k_ref in k_refs:
        z = _dot_nt(q_rows[0], head_rows(k_ref, 0))
        for h in range(1, SB_HEADS):
            z = z + _dot_nt(q_rows[h], head_rows(k_ref, h))
        zs.append(z)
    bias = jnp.concatenate([bias_ref[...]] * n_pg, axis=0)
    log_rem, log_beta = _log_stick(jnp.concatenate(zs, axis=0) * SCALE + bias)
    suffix, total = _suffix_and_total(log_rem, _suffix_matrix())
    later = jnp.where(group_k == 0, 0.0, later_ref[...])
    parts = []
    for i in range(n_pg):
        parts.append(suffix[i * SB_HEADS:(i + 1) * SB_HEADS] + later)
        later = later + total[i * SB_HEADS:(i + 1) * SB_HEADS]
    later_ref[...] = later
    a_ref[...] = jnp.exp(log_beta + jnp.concatenate(parts, axis=0))

    @pl.when((s > 0) & (group_v == n_groups - 1))
    def _():
        o_ref[0] = _rms(acc_ref[...], 1.0) * gain_ref[...]


def sb_decode(q, cache_k, cache_v, layer, page_table, bias, gain):
    n, n_pages = page_table.shape
    n_pg = DECODE_PAGES_PER_STEP
    n_groups, rem = divmod(n_pages, n_pg)
    assert rem == 0
    n_steps = n * n_groups
    rows = PAGE_SIZE * SB_HEADS
    depth, n_pool = cache_k.shape[:2]
    ck = cache_k.reshape(depth, n_pool, rows, HEAD_DIM)
    cv = cache_v.reshape(depth, n_pool, rows, HEAD_DIM)
    step_k = lambda s: jnp.minimum(s, n_steps - 1)
    step_v = lambda s: jnp.maximum(s - 1, 0)

    def page(step, i_pg):
        def index(s, pt):
            t = step(s)
            return layer, pt[t // n_groups, n_pages - 1 - (n_pg * (t % n_groups) + i_pg)], 0, 0
        return pl.BlockSpec((1, 1, rows, HEAD_DIM), index)

    per_head = pl.BlockSpec((SB_HEADS, HEAD_DIM), lambda s, pt: (0, 0))
    seq_block = lambda step: pl.BlockSpec((1, SB_HEADS, HEAD_DIM), lambda s, pt: (step(s) // n_groups, 0, 0))
    state = pltpu.VMEM((SB_HEADS, HEAD_DIM), F32)
    return pl.pallas_call(
        functools.partial(_sb_decode_kernel, n_groups),
        grid_spec=pltpu.PrefetchScalarGridSpec(
            num_scalar_prefetch=1,
            grid=(n_steps + 1,),
            in_specs=([seq_block(step_k)] + [page(step_k, i) for i in range(n_pg)]
                      + [page(step_v, i) for i in range(n_pg)] + [per_head, per_head]),
            out_specs=seq_block(step_v),
            scratch_shapes=[state, state, pltpu.VMEM((n_pg * SB_HEADS, HEAD_DIM), F32)]),
        out_shape=jax.ShapeDtypeStruct((n, SB_HEADS, HEAD_DIM), F32),
        compiler_params=_params("arbitrary"),
        name="sb_decode",
    )(page_table, q, *([ck] * n_pg), *([cv] * n_pg),
      jnp.broadcast_to(bias[:, None], (SB_HEADS, HEAD_DIM)),
      gain.reshape(SB_HEADS, HEAD_DIM))


def _rotate(x, cos2, sin2):
    return x * cos2 + pltpu.roll(x, HEAD_DIM // 2, 1) * sin2


def _retention_kernel(q_ref, k_ref, v_ref, g_ref, cos_ref, sin_ref, lg_ref, s_ref, o_ref, s_out_ref):
    chunk = float(HEAD_DIM)
    n_chunks = q_ref.shape[1] // HEAD_DIM
    lg = lg_ref[0]
    r = lax.broadcasted_iota(jnp.int32, (HEAD_DIM, HEAD_DIM), 0).astype(F32)
    c = lax.broadcasted_iota(jnp.int32, (HEAD_DIM, HEAD_DIM), 1).astype(F32)
    diff = r - c
    dmat = jnp.where(diff >= 0, jnp.exp(jnp.maximum(diff, 0.0) * lg), 0.0)
    cross = jnp.exp((r + 1.0) * lg)
    k_decay = jnp.exp((chunk - 1.0 - r) * lg)
    s_decay = jnp.exp(chunk * lg)

    def body(i, s):
        start = pl.multiple_of(i * HEAD_DIM, HEAD_DIM)
        rows = pl.ds(start, HEAD_DIM)
        cos2 = cos_ref[rows, :]
        sin2 = sin_ref[rows, :]
        q = _rotate(q_ref[0, rows, :], cos2, sin2).astype(BF16)
        k = _rotate(k_ref[0, rows, :], cos2, sin2) * SCALE
        v = v_ref[0, rows, :].astype(BF16)
        inner = _dot_nt(q, k.astype(BF16)) * dmat
        o = _dot(inner.astype(BF16), v) + _dot(q, s.astype(BF16)) * cross
        g = g_ref[0, rows, :]
        o_ref[0, rows, :] = (g * jax.nn.sigmoid(g) * _rms(o, 1.0)).astype(o_ref.dtype)
        return s_decay * s + _dot((k * k_decay).T.astype(BF16), v)

    s_out_ref[0, 0] = lax.fori_loop(0, n_chunks, body, s_ref[0, 0])


def retention(proj3, cos2, sin2, log_gamma, state):
    b, rows, _ = proj3.shape
    col = lambda off: pl.BlockSpec((1, rows, HEAD_DIM), lambda i, h: (i, 0, off + h))
    table = pl.BlockSpec((rows, HEAD_DIM), lambda i, h: (0, 0))
    st = pl.BlockSpec((1, 1, HEAD_DIM, HEAD_DIM), lambda i, h: (i, h, 0, 0))
    return pl.pallas_call(
        _retention_kernel,
        grid=(b, RET_HEADS),
        in_specs=[col(COL_RQ), col(COL_RK), col(COL_RV), col(COL_RG), table, table,
                  pl.BlockSpec((1, 1, HEAD_DIM), lambda i, h: (h, 0, 0)), st],
        out_specs=[pl.BlockSpec((1, rows, HEAD_DIM), lambda i, h: (i, 0, h)), st],
        out_shape=[jax.ShapeDtypeStruct((b, rows, RET_DIM), BF16),
                   jax.ShapeDtypeStruct((b, RET_HEADS, HEAD_DIM, HEAD_DIM), F32)],
        compiler_params=_params("parallel", "parallel"),
        name="retention",
    )(proj3, proj3, proj3, proj3, cos2, sin2,
      jnp.broadcast_to(log_gamma[:, None, None], (RET_HEADS, 1, HEAD_DIM)), state)


RET_STEP_ROWS = 8


def _retention_step_kernel(q_ref, k_ref, v_ref, g_ref, cos_ref, sin_ref, lg_ref, s_ref, o_ref, s_out_ref):
    n_rows = q_ref.shape[0]
    cos2, sin2 = cos_ref[...], sin_ref[...]
    fill = jnp.zeros((HEAD_DIM - n_rows, HEAD_DIM), F32)
    columns = lambda x: jnp.concatenate([x, fill], axis=0).T
    for h in range(RET_HEADS):
        lanes = slice(h * HEAD_DIM, (h + 1) * HEAD_DIM)
        gamma = jnp.exp(lg_ref[h])
        q = _rotate(q_ref[:, lanes], cos2, sin2)
        k = _rotate(k_ref[:, lanes], cos2, sin2) * SCALE
        v = v_ref[:, lanes]
        inner = jnp.sum(q * k, axis=-1, keepdims=True)
        q_cols, k_cols = columns(q), columns(k)
        outs = []
        for r in range(n_rows):
            state = s_ref[r, h]
            v_row = v[r:r + 1]
            q_state = jnp.sum(q_cols[:, r:r + 1] * state, axis=0, keepdims=True)
            outs.append(inner[r:r + 1] * v_row + q_state * gamma)
            s_out_ref[r, h] = gamma * state + k_cols[:, r:r + 1] * v_row
        g = g_ref[:, lanes]
        o_ref[:, lanes] = g * jax.nn.sigmoid(g) * _rms(jnp.concatenate(outs, axis=0), 1.0)


def retention_step(proj, cos2, sin2, log_gamma, state):
    n = proj.shape[0]
    rows = RET_STEP_ROWS
    col = lambda j: pl.BlockSpec((rows, RET_DIM), lambda i: (i, COL_RQ * HEAD_DIM // RET_DIM + j))
    table = pl.BlockSpec((1, HEAD_DIM), lambda i: (0, 0))
    st = pl.BlockSpec((rows, RET_HEADS, HEAD_DIM, HEAD_DIM), lambda i: (i, 0, 0, 0))
    return pl.pallas_call(
        _retention_step_kernel,
        grid=(n // rows,),
        in_specs=[col(0), col(1), col(2), col(3), table, table,
                  pl.BlockSpec((RET_HEADS, 1, HEAD_DIM), lambda i: (0, 0, 0)), st],
        out_specs=[pl.BlockSpec((rows, RET_DIM), lambda i: (i, 0)), st],
        out_shape=[jax.ShapeDtypeStruct((n, RET_DIM), F32),
                   jax.ShapeDtypeStruct((n, RET_HEADS, HEAD_DIM, HEAD_DIM), F32)],
        compiler_params=_params("parallel"),
        name="retention_step",
    )(proj, proj, proj, proj, cos2, sin2,
      jnp.broadcast_to(log_gamma[:, None, None], (RET_HEADS, 1, HEAD_DIM)), state)


def _rotary_tables(pos):
    inv = ROPE_BASE ** (-jnp.arange(0, HEAD_DIM, 2, dtype=F32) / HEAD_DIM)
    ang = pos.astype(F32)[:, None] * inv[None, :]
    cos, sin = jnp.cos(ang), jnp.sin(ang)
    return jnp.concatenate([cos, cos], axis=-1), jnp.concatenate([-sin, sin], axis=-1)


def _tiles(m):
    return {"tm": min(m, 512), "tn_in": IN_COLS // 4, "tf": 512, "tk_down": FFN_DIM // 4}


def _log_gamma():
    return jnp.log1p(-jnp.exp2(-5.0 - jnp.arange(RET_HEADS, dtype=F32)))


def _mix_prompt(xn, b, seq, p, layer, depth, kv_all):
    t = _tiles(b * seq)
    proj3 = in_proj(xn, p["w_in"], t["tm"], t["tn_in"]).reshape(b, seq, IN_COLS)
    conv_out, conv_new = conv_prompt(proj3, jnp.zeros((b, CONV_WIDTH - 1, CONV_DIM), F32), p["conv_w"])
    sb_out, kv_all = sb_prompt(proj3, p["sb_bias"], p["sb_gain"], layer, depth, kv_all)
    cos2, sin2 = _rotary_tables(jnp.arange(seq, dtype=jnp.int32))
    ret_out, ret_new = retention(proj3, cos2, sin2, _log_gamma(),
                                 jnp.zeros((b, RET_HEADS, HEAD_DIM, HEAD_DIM), F32))
    mix = (conv_out.reshape(b * seq, CONV_DIM), sb_out.reshape(b * seq, SB_DIM), ret_out.reshape(b * seq, RET_DIM))
    return mix, kv_all, conv_new, ret_new


def _mix_sample(xn, p, cache_k, cache_v, layer, state_conv, state_ret, page_table):
    m = xn.shape[0]
    t = _tiles(m)
    proj = in_proj(xn, p["w_in"], t["tm"], t["tn_in"])
    conv_out, conv_new = conv_step(proj, state_conv, p["conv_w"])
    heads = lambda col: proj[:, col * HEAD_DIM:(col + SB_HEADS) * HEAD_DIM].reshape(m, SB_HEADS, HEAD_DIM)
    sb_out = sb_decode(heads(COL_SQ), cache_k, cache_v, layer, page_table, p["sb_bias"], p["sb_gain"])
    past_len = page_table.shape[1] * PAGE_SIZE
    cos2, sin2 = _rotary_tables(jnp.full((1,), past_len, jnp.int32))
    ret_out, ret_new = retention_step(proj, cos2, sin2, _log_gamma(), state_ret)
    mix = (conv_out, sb_out.reshape(m, SB_DIM).astype(BF16), ret_out.astype(BF16))
    return mix, heads(COL_SK)[:, None], heads(COL_SV)[:, None], conv_new, ret_new


def _finish_layer(mix, x, p, g_next):
    t = _tiles(x.shape[0])
    x, xn = out_proj(*mix, p["w_out"], x, p["g_mix_post"], p["g_ffn_pre"], t["tm"])
    h = gate_up(xn, p["w_gate_up"], t["tm"], t["tf"])
    return down_proj(h, p["w_down"], x, p["g_ffn_post"], g_next, t["tm"], t["tk_down"])


def kernel(x_prompt, x_sample, cache_k, cache_v, state_conv, state_ret, page_table, norm_mix_pre, norm_mix_post, norm_ffn_pre, norm_ffn_post, w_in, conv_w, sb_bias, sb_gain, w_out, w_gate_up, w_down):
    b, seq, _ = x_prompt.shape
    n = x_sample.shape[0]
    depth = w_in.shape[0]
    xp = x_prompt.reshape(b * seq, D_MODEL)
    xs = x_sample.reshape(n, D_MODEL)
    xpn = norm_cast(xp, norm_mix_pre[0], _tiles(b * seq)["tm"])
    xsn = norm_cast(xs, norm_mix_pre[0], _tiles(n)["tm"])
    kv_all = None
    outs_p, outs_s = [], []
    for l in range(depth):
        p = {"w_in": w_in[l].astype(BF16), "w_out": w_out[l].astype(BF16),
             "w_gate_up": w_gate_up[l].astype(BF16), "w_down": w_down[l].astype(BF16),
             "conv_w": conv_w[l], "sb_bias": sb_bias[l], "sb_gain": sb_gain[l],
             "g_mix_post": norm_mix_post[l], "g_ffn_pre": norm_ffn_pre[l], "g_ffn_post": norm_ffn_post[l]}
        g_next = norm_mix_pre[l + 1] if l + 1 < depth else None
        mix, kv_all, *rest_p = _mix_prompt(xpn, b, seq, p, l, depth, kv_all)
        xp, xpn = _finish_layer(mix, xp, p, g_next)
        mix, *rest_s = _mix_sample(xsn, p, cache_k, cache_v, l, state_conv[l], state_ret[l], page_table)
        xs, xsn = _finish_layer(mix, xs, p, g_next)
        outs_p.append(rest_p)
        outs_s.append(rest_s)
    stack = lambda outs, i: jnp.stack([o[i] for o in outs])
    k_prompt, v_prompt = (a.reshape(depth, b, seq, SB_HEADS, HEAD_DIM) for a in kv_all)
    return (xp.reshape(b, seq, D_MODEL), xs.reshape(n, 1, D_MODEL),
            k_prompt, v_prompt, stack(outs_p, 0), stack(outs_p, 1),
            stack(outs_s, 0), stack(outs_s, 1), stack(outs_s, 2), stack(outs_s, 3))
```

```python
import functools

import jax
import jax.numpy as jnp
from jax import lax
from jax.experimental import pallas as pl
from jax.experimental.pallas import tpu as pltpu

D_MODEL = 2048
HEAD_DIM = 128
CONV_DIM = 512
SB_DIM = 1024
RET_DIM = 512
SB_HEADS = 8
RET_HEADS = 4
CONV_WIDTH = 3
IN_COLS = 3 * CONV_DIM + 3 * SB_DIM + 4 * RET_DIM
FFN_DIM = 5632
PAGE_SIZE = 128
ROPE_BASE = 10000.0
RMS_EPS = 1e-6
SCALE = HEAD_DIM ** -0.5

COL_CB, COL_CC, COL_CH = 0, 4, 8
COL_SQ, COL_SK, COL_SV = 12, 20, 28
COL_RQ, COL_RK, COL_RV, COL_RG = 36, 40, 44, 48

VMEM_LIMIT_BYTES = 56 * 1024 * 1024
BF16 = jnp.bfloat16
F32 = jnp.float32


def _params(*semantics):
    return pltpu.CompilerParams(dimension_semantics=semantics, vmem_limit_bytes=VMEM_LIMIT_BYTES)


def _rms(x, gain):
    return x * lax.rsqrt(jnp.mean(x * x, axis=-1, keepdims=True) + RMS_EPS) * gain


def _log_stick(z):
    tail = jnp.log(1.0 + jnp.exp(-jnp.abs(z)))
    return jnp.minimum(-z, 0.0) - tail, jnp.minimum(z, 0.0) - tail


def _dot(a, b):
    return jnp.dot(a, b, preferred_element_type=F32)


def _dot_nt(a, b):
    return lax.dot_general(a, b, (((1,), (1,)), ((), ())), preferred_element_type=F32)


def _suffix_matrix():
    r = lax.broadcasted_iota(jnp.int32, (2 * HEAD_DIM, 2 * HEAD_DIM), 0) % HEAD_DIM
    c = lax.broadcasted_iota(jnp.int32, (2 * HEAD_DIM, 2 * HEAD_DIM), 1)
    return jnp.where((r > c) | (c >= HEAD_DIM), 1.0, 0.0).astype(BF16)


def _suffix_and_total(log_rem, tri):
    hi = log_rem.astype(BF16)
    lo = (log_rem - hi.astype(F32)).astype(BF16)
    s = _dot(jnp.concatenate([hi, lo], axis=1), tri)
    return s[:, :HEAD_DIM], s[:, HEAD_DIM:]


def _norm_cast_kernel(x_ref, g_ref, o_ref):
    o_ref[...] = _rms(x_ref[...], g_ref[...]).astype(BF16)


def norm_cast(x, gain, tm):
    m = x.shape[0]
    return pl.pallas_call(
        _norm_cast_kernel,
        grid=(m // tm,),
        in_specs=[pl.BlockSpec((tm, D_MODEL), lambda i: (i, 0)),
                  pl.BlockSpec((1, D_MODEL), lambda i: (0, 0))],
        out_specs=pl.BlockSpec((tm, D_MODEL), lambda i: (i, 0)),
        out_shape=jax.ShapeDtypeStruct((m, D_MODEL), BF16),
        compiler_params=_params("parallel"),
        name="norm_cast",
    )(x, gain.reshape(1, D_MODEL))


def _in_proj_kernel(a_ref, a2_ref, w_ref, o_ref, o2_ref, wb_ref):
    @pl.when(pl.program_id(1) == 0)
    def _():
        wb_ref[...] = w_ref[...].astype(BF16)
        o2_ref[...] = _dot(a2_ref[...], wb_ref[...])

    o_ref[...] = _dot(a_ref[...], wb_ref[...])


def in_proj(a, a2, w, layer, tm, tn):
    m, k = a.shape
    m2 = a2.shape[0]
    n = w.shape[2]
    return pl.pallas_call(
        _in_proj_kernel,
        grid=(n // tn, m // tm),
        in_specs=[pl.BlockSpec((tm, k), lambda j, i: (i, 0)),
                  pl.BlockSpec((m2, k), lambda j, i: (0, 0)),
                  pl.BlockSpec((None, k, tn), lambda j, i: (layer, 0, j))],
        out_specs=[pl.BlockSpec((tm, tn), lambda j, i: (i, j)),
                   pl.BlockSpec((m2, tn), lambda j, i: (0, j))],
        out_shape=[jax.ShapeDtypeStruct((m, n), F32), jax.ShapeDtypeStruct((m2, n), F32)],
        scratch_shapes=[pltpu.VMEM((k, tn), BF16)],
        compiler_params=_params("arbitrary", "arbitrary"),
        name="in_proj",
    )(a, a2, w)


def _out_proj_kernel(c_ref, s_ref, r_ref, w_ref, x_ref, gpost_ref, gnext_ref, x_out_ref, xn_out_ref):
    mix = jnp.concatenate([c_ref[...], s_ref[...], r_ref[...]], axis=-1)
    x = x_ref[...] + _rms(_dot(mix, w_ref[...]), gpost_ref[...])
    x_out_ref[...] = x
    xn_out_ref[...] = _rms(x, gnext_ref[...]).astype(BF16)


def out_proj(conv_out, sb_out, ret_out, w, layer, x, g_post, g_next, tm):
    m = x.shape[0]
    row = lambda i: (i, 0)
    fixed = lambda i: (0, 0)
    return pl.pallas_call(
        _out_proj_kernel,
        grid=(m // tm,),
        in_specs=[pl.BlockSpec((tm, CONV_DIM), row),
                  pl.BlockSpec((tm, SB_DIM), row),
                  pl.BlockSpec((tm, RET_DIM), row),
                  pl.BlockSpec((None, D_MODEL, D_MODEL), lambda i: (layer, 0, 0)),
                  pl.BlockSpec((tm, D_MODEL), row),
                  pl.BlockSpec((1, D_MODEL), fixed),
                  pl.BlockSpec((1, D_MODEL), fixed)],
        out_specs=[pl.BlockSpec((tm, D_MODEL), row), pl.BlockSpec((tm, D_MODEL), row)],
        out_shape=[jax.ShapeDtypeStruct((m, D_MODEL), F32), jax.ShapeDtypeStruct((m, D_MODEL), BF16)],
        compiler_params=_params("parallel"),
        name="out_proj",
    )(conv_out, sb_out, ret_out, w, x, g_post.reshape(1, -1), g_next.reshape(1, -1))


def _gate_up_kernel(a_ref, a2_ref, wg_ref, wu_ref, o_ref, o2_ref, wgb_ref, wub_ref):
    def swiglu(a):
        gate = _dot(a, wgb_ref[...])
        up = _dot(a, wub_ref[...])
        return (gate * jax.nn.sigmoid(gate) * up).astype(BF16)

    @pl.when(pl.program_id(1) == 0)
    def _():
        wgb_ref[...] = wg_ref[...].astype(BF16)
        wub_ref[...] = wu_ref[...].astype(BF16)
        o2_ref[...] = swiglu(a2_ref[...])

    o_ref[...] = swiglu(a_ref[...])


def gate_up(a, a2, w, layer, tm, tf):
    m, m2 = a.shape[0], a2.shape[0]
    nf = FFN_DIM // tf
    return pl.pallas_call(
        _gate_up_kernel,
        grid=(nf, m // tm),
        in_specs=[pl.BlockSpec((tm, D_MODEL), lambda j, i: (i, 0)),
                  pl.BlockSpec((m2, D_MODEL), lambda j, i: (0, 0)),
                  pl.BlockSpec((None, D_MODEL, tf), lambda j, i: (layer, 0, j)),
                  pl.BlockSpec((None, D_MODEL, tf), lambda j, i: (layer, 0, j + nf))],
        out_specs=[pl.BlockSpec((tm, tf), lambda j, i: (i, j)),
                   pl.BlockSpec((m2, tf), lambda j, i: (0, j))],
        out_shape=[jax.ShapeDtypeStruct((m, FFN_DIM), BF16), jax.ShapeDtypeStruct((m2, FFN_DIM), BF16)],
        scratch_shapes=[pltpu.VMEM((D_MODEL, tf), BF16), pltpu.VMEM((D_MODEL, tf), BF16)],
        compiler_params=_params("arbitrary", "arbitrary"),
        name="gate_up",
    )(a, a2, w, w)


def _down_kernel(emit_next, h_ref, w_ref, x_ref, gpost_ref, *rest):
    if emit_next:
        gnext_ref, x_out_ref, xn_out_ref, acc_ref = rest
    else:
        x_out_ref, acc_ref = rest
    k = pl.program_id(1)

    @pl.when(k == 0)
    def _():
        acc_ref[...] = jnp.zeros_like(acc_ref)

    acc_ref[...] += _dot(h_ref[...], w_ref[...])

    @pl.when(k == pl.num_programs(1) - 1)
    def _():
        x = x_ref[...] + _rms(acc_ref[...], gpost_ref[...])
        x_out_ref[...] = x
        if emit_next:
            xn_out_ref[...] = _rms(x, gnext_ref[...]).astype(BF16)


def down_proj(h, w, layer, x, g_post, g_next, tm, tk):
    m = x.shape[0]
    emit_next = g_next is not None
    row = lambda i, k: (i, 0)
    fixed = lambda i, k: (0, 0)
    in_specs = [pl.BlockSpec((tm, tk), lambda i, k: (i, k)),
                pl.BlockSpec((None, tk, D_MODEL), lambda i, k: (layer, k, 0)),
                pl.BlockSpec((tm, D_MODEL), row),
                pl.BlockSpec((1, D_MODEL), fixed)]
    args = [h, w, x, g_post.reshape(1, -1)]
    out_specs = [pl.BlockSpec((tm, D_MODEL), row)]
    out_shape = [jax.ShapeDtypeStruct((m, D_MODEL), F32)]
    if emit_next:
        in_specs.append(pl.BlockSpec((1, D_MODEL), fixed))
        args.append(g_next.reshape(1, -1))
        out_specs.append(pl.BlockSpec((tm, D_MODEL), row))
        out_shape.append(jax.ShapeDtypeStruct((m, D_MODEL), BF16))
    outs = pl.pallas_call(
        functools.partial(_down_kernel, emit_next),
        grid=(m // tm, FFN_DIM // tk),
        in_specs=in_specs,
        out_specs=out_specs,
        out_shape=out_shape,
        scratch_shapes=[pltpu.VMEM((tm, D_MODEL), F32)],
        compiler_params=_params("parallel", "arbitrary"),
        name="down_proj",
    )(*args)
    return (outs[0], outs[1]) if emit_next else (outs[0], None)


def _conv_prompt_kernel(cb_ref, cc_ref, ch_ref, st_ref, w_ref, o_ref, new_ref):
    u = cc_ref[0] * ch_ref[0]
    seq = u.shape[0]
    st = st_ref[0]
    w = w_ref[...]
    row = lax.broadcasted_iota(jnp.int32, u.shape, 0)
    back1 = jnp.where(row == 0, st[1:2], pltpu.roll(u, 1, 0))
    back2 = jnp.where(row == 0, st[0:1], jnp.where(row == 1, st[1:2], pltpu.roll(u, 2, 0)))
    y = back2 * w[0:1] + back1 * w[1:2] + u * w[2:3]
    o_ref[0] = (cb_ref[0] * y).astype(BF16)
    new_ref[0] = u[seq - 2:, :]


def conv_prompt(proj3, state, conv_w):
    b, seq, _ = proj3.shape
    col = lambda off: pl.BlockSpec((1, seq, HEAD_DIM), lambda i, c: (i, 0, off + c))
    return pl.pallas_call(
        _conv_prompt_kernel,
        grid=(b, CONV_DIM // HEAD_DIM),
        in_specs=[col(COL_CB), col(COL_CC), col(COL_CH),
                  pl.BlockSpec((1, CONV_WIDTH - 1, HEAD_DIM), lambda i, c: (i, 0, c)),
                  pl.BlockSpec((CONV_WIDTH, HEAD_DIM), lambda i, c: (0, c))],
        out_specs=[pl.BlockSpec((1, seq, HEAD_DIM), lambda i, c: (i, 0, c)),
                   pl.BlockSpec((1, CONV_WIDTH - 1, HEAD_DIM), lambda i, c: (i, 0, c))],
        out_shape=[jax.ShapeDtypeStruct((b, seq, CONV_DIM), BF16),
                   jax.ShapeDtypeStruct((b, CONV_WIDTH - 1, CONV_DIM), F32)],
        compiler_params=_params("parallel", "parallel"),
        name="conv_prompt",
    )(proj3, proj3, proj3, state, conv_w)


def _conv_step_kernel(cb_ref, cc_ref, ch_ref, s0_ref, s1_ref, w_ref, o_ref, u_ref):
    u = cc_ref[...] * ch_ref[...]
    w = w_ref[...]
    y = s0_ref[...] * w[0:1] + s1_ref[...] * w[1:2] + u * w[2:3]
    o_ref[...] = (cb_ref[...] * y).astype(BF16)
    u_ref[...] = u


def conv_step(proj, state, conv_w):
    n = proj.shape[0]
    col = lambda j: pl.BlockSpec((n, CONV_DIM), lambda i: (0, j))
    full = pl.BlockSpec((n, CONV_DIM), lambda i: (0, 0))
    out, u = pl.pallas_call(
        _conv_step_kernel,
        grid=(1,),
        in_specs=[col(0), col(1), col(2), full, full,
                  pl.BlockSpec((CONV_WIDTH, CONV_DIM), lambda i: (0, 0))],
        out_specs=[full, full],
        out_shape=[jax.ShapeDtypeStruct((n, CONV_DIM), BF16), jax.ShapeDtypeStruct((n, CONV_DIM), F32)],
        compiler_params=_params("arbitrary"),
        name="conv_step",
    )(proj, proj, proj, state[:, 0], state[:, 1], conv_w)
    return out, jnp.stack([state[:, 1], u], axis=1)


def _sb_prompt_kernel(q_ref, k_ref, v_ref, bias_ref, gain_ref, *rest):
    o_ref, k_out_ref, v_out_ref = rest[-3:]
    qi = pl.program_id(2)
    tq = q_ref.shape[1]

    @pl.when(qi == 0)
    def _():
        k_out_ref[...] = k_ref[...]
        v_out_ref[...] = v_ref[...]

    q = q_ref[0].astype(BF16)
    bias = bias_ref[0]
    tri = _suffix_matrix()
    n_blk = tq // HEAD_DIM
    bias = jnp.concatenate([bias] * n_blk, axis=1)

    n_str = 2
    rows = tq // n_str
    qs = [q[i * rows:(i + 1) * rows] for i in range(n_str)]

    def key_group(g, carry, masked):
        accs, laters = carry
        start = pl.multiple_of(g * tq, tq)
        kg = k_ref[0, pl.ds(start, tq), :].astype(BF16)
        vg = v_ref[0, pl.ds(start, tq), :].astype(BF16)
        zs = [_dot_nt(qs[i], kg) for i in range(n_str)]
        stacks, betas, masks = [], [], []
        for i in range(n_str):
            log_rem, log_beta = _log_stick(zs[i] * SCALE + bias)
            if masked:
                mask = (lax.broadcasted_iota(jnp.int32, (rows, tq), 1)
                        < lax.broadcasted_iota(jnp.int32, (rows, tq), 0) + i * rows)
                log_rem = jnp.where(mask, log_rem, 0.0)
                masks.append(mask)
            blocks = [log_rem[:, j * HEAD_DIM:(j + 1) * HEAD_DIM] for j in range(n_blk)]
            stacks.append(jnp.concatenate(blocks, axis=0))
            betas.append(log_beta)
        sums = [_suffix_and_total(stacks[i], tri) for i in range(n_str)]
        new_accs, new_laters = [], []
        weights = []
        for i in range(n_str):
            suffix, total = sums[i]
            later = laters[i]
            parts = [None] * n_blk
            for j in reversed(range(n_blk)):
                parts[j] = suffix[j * rows:(j + 1) * rows] + later
                later = later + total[j * rows:(j + 1) * rows]
            a = jnp.exp(betas[i] + jnp.concatenate(parts, axis=1))
            if masked:
                a = jnp.where(masks[i], a, 0.0)
            weights.append(a.astype(BF16))
            new_laters.append(later)
        for i in range(n_str):
            new_accs.append(accs[i] + _dot(weights[i], vg))
        return tuple(new_accs), tuple(new_laters)

    zero = tuple(jnp.zeros((rows, HEAD_DIM), F32) for _ in range(n_str))
    carry = key_group(qi, (zero, zero), True)
    carry = lax.fori_loop(0, qi, lambda t, c: key_group(qi - 1 - t, c, False), carry)
    o_ref[0] = (_rms(jnp.concatenate(carry[0], axis=0), 1.0) * gain_ref[0]).astype(BF16)


def sb_prompt(proj3, bias, gain, layer, depth, kv_all):
    b, seq, _ = proj3.shape
    tq = 4 * HEAD_DIM
    kv = lambda off: pl.BlockSpec((1, seq, HEAD_DIM), lambda i, h, q: (i, 0, off + h))
    per_head = pl.BlockSpec((1, 1, HEAD_DIM), lambda i, h, q: (h, 0, 0))
    kv_out = pl.BlockSpec((None, 1, seq, HEAD_DIM), lambda i, h, q: (layer, i, 0, h))
    kv_shape = jax.ShapeDtypeStruct((depth, b, seq, SB_DIM), F32)
    in_specs = [pl.BlockSpec((1, tq, HEAD_DIM), lambda i, h, q: (i, q, COL_SQ + h)),
                kv(COL_SK), kv(COL_SV), per_head, per_head]
    args = [proj3, proj3, proj3,
            jnp.broadcast_to(bias[:, None, None], (SB_HEADS, 1, HEAD_DIM)),
            gain.reshape(SB_HEADS, 1, HEAD_DIM)]
    aliases = {}
    if kv_all is not None:
        aliases = {len(args): 1, len(args) + 1: 2}
        in_specs += [pl.BlockSpec(memory_space=pl.ANY)] * 2
        args += list(kv_all)
    sb_out, k_all, v_all = pl.pallas_call(
        _sb_prompt_kernel,
        grid=(b, SB_HEADS, seq // tq),
        in_specs=in_specs,
        out_specs=[pl.BlockSpec((1, tq, HEAD_DIM), lambda i, h, q: (i, q, h)), kv_out, kv_out],
        out_shape=[jax.ShapeDtypeStruct((b, seq, SB_DIM), BF16), kv_shape, kv_shape],
        input_output_aliases=aliases,
        compiler_params=_params("parallel", "parallel", "arbitrary"),
        name="sb_prompt",
    )(*args)
    return sb_out, (k_all, v_all)


DECODE_PAGES_PER_STEP = 16


def _sb_decode_kernel(n_groups, pt_ref, q_ref, *refs):
    del pt_ref
    n_pg = DECODE_PAGES_PER_STEP
    k_refs, v_refs = refs[:n_pg], refs[n_pg:2 * n_pg]
    bias_ref, gain_ref, o_ref, acc_ref, later_ref, a_ref = refs[2 * n_pg:]
    s = pl.program_id(0)
    last = pl.num_programs(0) - 1
    group_k = jnp.minimum(s, last - 1) % n_groups
    group_v = jnp.maximum(s - 1, 0) % n_groups

    @pl.when(s == 0)
    def _():
        acc_ref[...] = jnp.zeros_like(acc_ref)
        later_ref[...] = jnp.zeros_like(later_ref)
        a_ref[...] = jnp.zeros_like(a_ref)

    head = lax.broadcasted_iota(jnp.int32, (SB_HEADS, HEAD_DIM), 0)
    only = lambda x, h: jnp.where(head == h, x, 0.0).astype(BF16)
    head_rows = lambda ref, h: ref[0, 0, pl.ds(h, PAGE_SIZE, stride=SB_HEADS), :].astype(BF16)

    a_prev = a_ref[...]
    acc = jnp.where(group_v == 0, 0.0, acc_ref[...])
    for i, v_ref in enumerate(v_refs):
        a_page = a_prev[i * SB_HEADS:(i + 1) * SB_HEADS]
        out = _dot(only(a_page, 0), head_rows(v_ref, 0))
        for h in range(1, SB_HEADS):
            out = out + _dot(only(a_page, h), head_rows(v_ref, h))
        acc = acc + out
    acc_ref[...] = acc

    q = q_ref[0]
    q_rows = [only(q, h) for h in range(SB_HEADS)]
    zs = []
    for k_ref in k_refs:
        z = _dot_nt(q_rows[0], head_rows(k_ref, 0))
        for h in range(1, SB_HEADS):
            z = z + _dot_nt(q_rows[h], head_rows(k_ref, h))
        zs.append(z)
    bias = jnp.concatenate([bias_ref[...]] * n_pg, axis=0)
    log_rem, log_beta = _log_stick(jnp.concatenate(zs, axis=0) * SCALE + bias)
    suffix, total = _suffix_and_total(log_rem, _suffix_matrix())
    later = jnp.where(group_k == 0, 0.0, later_ref[...])
    parts = []
    for i in range(n_pg):
        parts.append(suffix[i * SB_HEADS:(i + 1) * SB_HEADS] + later)
        later = later + total[i * SB_HEADS:(i + 1) * SB_HEADS]
    later_ref[...] = later
    a_ref[...] = jnp.exp(log_beta + jnp.concatenate(parts, axis=0))

    @pl.when((s > 0) & (group_v == n_groups - 1))
    def _():
        o_ref[0] = _rms(acc_ref[...], 1.0) * gain_ref[...]


def sb_decode(q, cache_k, cache_v, layer, page_table, bias, gain):
    n, n_pages = page_table.shape
    n_pg = DECODE_PAGES_PER_STEP
    n_groups, rem = divmod(n_pages, n_pg)
    assert rem == 0
    n_steps = n * n_groups
    rows = PAGE_SIZE * SB_HEADS
    depth, n_pool = cache_k.shape[:2]
    ck = cache_k.reshape(depth, n_pool, rows, HEAD_DIM)
    cv = cache_v.reshape(depth, n_pool, rows, HEAD_DIM)
    step_k = lambda s: jnp.minimum(s, n_steps - 1)
    step_v = lambda s: jnp.maximum(s - 1, 0)

    def page(step, i_pg):
        def index(s, pt):
            t = step(s)
            return layer, pt[t // n_groups, n_pages - 1 - (n_pg * (t % n_groups) + i_pg)], 0, 0
        return pl.BlockSpec((1, 1, rows, HEAD_DIM), index)

    per_head = pl.BlockSpec((SB_HEADS, HEAD_DIM), lambda s, pt: (0, 0))
    seq_block = lambda step: pl.BlockSpec((1, SB_HEADS, HEAD_DIM), lambda s, pt: (step(s) // n_groups, 0, 0))
    state = pltpu.VMEM((SB_HEADS, HEAD_DIM), F32)
    return pl.pallas_call(
        functools.partial(_sb_decode_kernel, n_groups),
        grid_spec=pltpu.PrefetchScalarGridSpec(
            num_scalar_prefetch=1,
            grid=(n_steps + 1,),
            in_specs=([seq_block(step_k)] + [page(step_k, i) for i in range(n_pg)]
                      + [page(step_v, i) for i in range(n_pg)] + [per_head, per_head]),
            out_specs=seq_block(step_v),
            scratch_shapes=[state, state, pltpu.VMEM((n_pg * SB_HEADS, HEAD_DIM), F32)]),
        out_shape=jax.ShapeDtypeStruct((n, SB_HEADS, HEAD_DIM), F32),
        compiler_params=_params("arbitrary"),
        name="sb_decode",
    )(page_table, q, *([ck] * n_pg), *([cv] * n_pg),
      jnp.broadcast_to(bias[:, None], (SB_HEADS, HEAD_DIM)),
      gain.reshape(SB_HEADS, HEAD_DIM))


def _rotate(x, cos2, sin2):
    return x * cos2 + pltpu.roll(x, HEAD_DIM // 2, 1) * sin2


def _retention_kernel(q_ref, k_ref, v_ref, g_ref, cos_ref, sin_ref, lg_ref, s_ref, o_ref, s_out_ref):
    chunk = float(HEAD_DIM)
    n_chunks = q_ref.shape[1] // HEAD_DIM
    lg = lg_ref[0]
    r = lax.broadcasted_iota(jnp.int32, (HEAD_DIM, HEAD_DIM), 0).astype(F32)
    c = lax.broadcasted_iota(jnp.int32, (HEAD_DIM, HEAD_DIM), 1).astype(F32)
    diff = r - c
    dmat = jnp.where(diff >= 0, jnp.exp(jnp.maximum(diff, 0.0) * lg), 0.0)
    cross = jnp.exp((r + 1.0) * lg)
    k_decay = jnp.exp((chunk - 1.0 - r) * lg)
    s_decay = jnp.exp(chunk * lg)

    def body(i, s):
        start = pl.multiple_of(i * HEAD_DIM, HEAD_DIM)
        rows = pl.ds(start, HEAD_DIM)
        cos2 = cos_ref[rows, :]
        sin2 = sin_ref[rows, :]
        q = _rotate(q_ref[0, rows, :], cos2, sin2).astype(BF16)
        k = _rotate(k_ref[0, rows, :], cos2, sin2) * SCALE
        v = v_ref[0, rows, :].astype(BF16)
        inner = _dot_nt(q, k.astype(BF16)) * dmat
        o = _dot(inner.astype(BF16), v) + _dot(q, s.astype(BF16)) * cross
        g = g_ref[0, rows, :]
        o_ref[0, rows, :] = (g * jax.nn.sigmoid(g) * _rms(o, 1.0)).astype(o_ref.dtype)
        return s_decay * s + _dot((k * k_decay).T.astype(BF16), v)

    s_out_ref[0, 0] = lax.fori_loop(0, n_chunks, body, s_ref[0, 0])


def retention(proj3, cos2, sin2, log_gamma, state):
    b, rows, _ = proj3.shape
    col = lambda off: pl.BlockSpec((1, rows, HEAD_DIM), lambda i, h: (i, 0, off + h))
    table = pl.BlockSpec((rows, HEAD_DIM), lambda i, h: (0, 0))
    st = pl.BlockSpec((1, 1, HEAD_DIM, HEAD_DIM), lambda i, h: (i, h, 0, 0))
    return pl.pallas_call(
        _retention_kernel,
        grid=(b, RET_HEADS),
        in_specs=[col(COL_RQ), col(COL_RK), col(COL_RV), col(COL_RG), table, table,
                  pl.BlockSpec((1, 1, HEAD_DIM), lambda i, h: (h, 0, 0)), st],
        out_specs=[pl.BlockSpec((1, rows, HEAD_DIM), lambda i, h: (i, 0, h)), st],
        out_shape=[jax.ShapeDtypeStruct((b, rows, RET_DIM), BF16),
                   jax.ShapeDtypeStruct((b, RET_HEADS, HEAD_DIM, HEAD_DIM), F32)],
        compiler_params=_params("parallel", "parallel"),
        name="retention",
    )(proj3, proj3, proj3, proj3, cos2, sin2,
      jnp.broadcast_to(log_gamma[:, None, None], (RET_HEADS, 1, HEAD_DIM)), state)


RET_STEP_ROWS = 8


def _retention_step_kernel(q_ref, k_ref, v_ref, g_ref, cos_ref, sin_ref, lg_ref, s_ref, o_ref, s_out_ref):
    n_rows = q_ref.shape[0]
    cos2, sin2 = cos_ref[...], sin_ref[...]
    fill = jnp.zeros((HEAD_DIM - n_rows, HEAD_DIM), F32)
    columns = lambda x: jnp.concatenate([x, fill], axis=0).T
    for h in range(RET_HEADS):
        lanes = slice(h * HEAD_DIM, (h + 1) * HEAD_DIM)
        gamma = jnp.exp(lg_ref[h])
        q = _rotate(q_ref[:, lanes], cos2, sin2)
        k = _rotate(k_ref[:, lanes], cos2, sin2) * SCALE
        v = v_ref[:, lanes]
        inner = jnp.sum(q * k, axis=-1, keepdims=True)
        q_cols, k_cols = columns(q), columns(k)
        outs = []
        for r in range(n_rows):
            state = s_ref[r, h]
            v_row = v[r:r + 1]
            q_state = jnp.sum(q_cols[:, r:r + 1] * state, axis=0, keepdims=True)
            outs.append(inner[r:r + 1] * v_row + q_state * gamma)
            s_out_ref[r, h] = gamma * state + k_cols[:, r:r + 1] * v_row
        g = g_ref[:, lanes]
        o_ref[:, lanes] = g * jax.nn.sigmoid(g) * _rms(jnp.concatenate(outs, axis=0), 1.0)


def retention_step(proj, cos2, sin2, log_gamma, state):
    n = proj.shape[0]
    rows = RET_STEP_ROWS
    col = lambda j: pl.BlockSpec((rows, RET_DIM), lambda i: (i, COL_RQ * HEAD_DIM // RET_DIM + j))
    table = pl.BlockSpec((1, HEAD_DIM), lambda i: (0, 0))
    st = pl.BlockSpec((rows, RET_HEADS, HEAD_DIM, HEAD_DIM), lambda i: (i, 0, 0, 0))
    return pl.pallas_call(
        _retention_step_kernel,
        grid=(n // rows,),
        in_specs=[col(0), col(1), col(2), col(3), table, table,
                  pl.BlockSpec((RET_HEADS, 1, HEAD_DIM), lambda i: (0, 0, 0)), st],
        out_specs=[pl.BlockSpec((rows, RET_DIM), lambda i: (i, 0)), st],
        out_shape=[jax.ShapeDtypeStruct((n, RET_DIM), F32),
                   jax.ShapeDtypeStruct((n, RET_HEADS, HEAD_DIM, HEAD_DIM), F32)],
        compiler_params=_params("parallel"),
        name="retention_step",
    )(proj, proj, proj, proj, cos2, sin2,
      jnp.broadcast_to(log_gamma[:, None, None], (RET_HEADS, 1, HEAD_DIM)), state)


def _rotary_tables(pos):
    inv = ROPE_BASE ** (-jnp.arange(0, HEAD_DIM, 2, dtype=F32) / HEAD_DIM)
    ang = pos.astype(F32)[:, None] * inv[None, :]
    cos, sin = jnp.cos(ang), jnp.sin(ang)
    return jnp.concatenate([cos, cos], axis=-1), jnp.concatenate([-sin, sin], axis=-1)


def _tiles(m):
    return {"tm": min(m, 512), "tm_wide": min(m, 1024), "tn_in": IN_COLS // 4, "tf": 512,
            "tk_down": FFN_DIM // 4}


def _log_gamma():
    return jnp.log1p(-jnp.exp2(-5.0 - jnp.arange(RET_HEADS, dtype=F32)))


def _mix_prompt(proj, b, seq, p, layer, depth, kv_all):
    proj3 = proj.reshape(b, seq, IN_COLS)
    conv_out, conv_new = conv_prompt(proj3, jnp.zeros((b, CONV_WIDTH - 1, CONV_DIM), F32), p["conv_w"])
    sb_out, kv_all = sb_prompt(proj3, p["sb_bias"], p["sb_gain"], layer, depth, kv_all)
    cos2, sin2 = _rotary_tables(jnp.arange(seq, dtype=jnp.int32))
    ret_out, ret_new = retention(proj3, cos2, sin2, _log_gamma(),
                                 jnp.zeros((b, RET_HEADS, HEAD_DIM, HEAD_DIM), F32))
    mix = (conv_out.reshape(b * seq, CONV_DIM), sb_out.reshape(b * seq, SB_DIM), ret_out.reshape(b * seq, RET_DIM))
    return mix, kv_all, conv_new, ret_new


def _mix_sample(proj, p, cache_k, cache_v, layer, state_conv, state_ret, page_table):
    m = proj.shape[0]
    conv_out, conv_new = conv_step(proj, state_conv, p["conv_w"])
    heads = lambda col: proj[:, col * HEAD_DIM:(col + SB_HEADS) * HEAD_DIM].reshape(m, SB_HEADS, HEAD_DIM)
    sb_out = sb_decode(heads(COL_SQ), cache_k, cache_v, layer, page_table, p["sb_bias"], p["sb_gain"])
    past_len = page_table.shape[1] * PAGE_SIZE
    cos2, sin2 = _rotary_tables(jnp.full((1,), past_len, jnp.int32))
    ret_out, ret_new = retention_step(proj, cos2, sin2, _log_gamma(), state_ret)
    mix = (conv_out, sb_out.reshape(m, SB_DIM).astype(BF16), ret_out.astype(BF16))
    return mix, heads(COL_SK)[:, None], heads(COL_SV)[:, None], conv_new, ret_new


def _mix_out(mix, x, p, layer):
    return out_proj(*mix, p["w_out"], layer, x, p["g_mix_post"], p["g_ffn_pre"], _tiles(x.shape[0])["tm"])


def _ffn_out(h, x, p, layer, g_next):
    t = _tiles(x.shape[0])
    return down_proj(h, p["w_down"], layer, x, p["g_ffn_post"], g_next, t["tm"], t["tk_down"])


def kernel(x_prompt, x_sample, cache_k, cache_v, state_conv, state_ret, page_table, norm_mix_pre, norm_mix_post, norm_ffn_pre, norm_ffn_post, w_in, conv_w, sb_bias, sb_gain, w_out, w_gate_up, w_down):
    b, seq, _ = x_prompt.shape
    n = x_sample.shape[0]
    depth = w_in.shape[0]
    xp = x_prompt.reshape(b * seq, D_MODEL)
    xs = x_sample.reshape(n, D_MODEL)
    xpn = norm_cast(xp, norm_mix_pre[0], _tiles(b * seq)["tm"])
    xsn = norm_cast(xs, norm_mix_pre[0], _tiles(n)["tm"])
    kv_all = None
    outs_p, outs_s = [], []
    w_out_bf, w_down_bf = w_out.astype(BF16), w_down.astype(BF16)
    t = _tiles(b * seq)
    for l in range(depth):
        p = {"w_out": w_out_bf, "w_down": w_down_bf,
             "conv_w": conv_w[l], "sb_bias": sb_bias[l], "sb_gain": sb_gain[l],
             "g_mix_post": norm_mix_post[l], "g_ffn_pre": norm_ffn_pre[l], "g_ffn_post": norm_ffn_post[l]}
        g_next = norm_mix_pre[l + 1] if l + 1 < depth else None
        proj_p, proj_s = in_proj(xpn, xsn, w_in, l, t["tm"], t["tn_in"])
        mix_p, kv_all, *rest_p = _mix_prompt(proj_p, b, seq, p, l, depth, kv_all)
        mix_s, *rest_s = _mix_sample(proj_s, p, cache_k, cache_v, l, state_conv[l], state_ret[l], page_table)
        xp, xpn = _mix_out(mix_p, xp, p, l)
        xs, xsn = _mix_out(mix_s, xs, p, l)
        hp, hs = gate_up(xpn, xsn, w_gate_up, l, t["tm_wide"], t["tf"])
        xp, xpn = _ffn_out(hp, xp, p, l, g_next)
        xs, xsn = _ffn_out(hs, xs, p, l, g_next)
        outs_p.append(rest_p)
        outs_s.append(rest_s)
    stack = lambda outs, i: jnp.stack([o[i] for o in outs])
    k_prompt, v_prompt = (a.reshape(depth, b, seq, SB_HEADS, HEAD_DIM) for a in kv_all)
    return (xp.reshape(b, seq, D_MODEL), xs.reshape(n, 1, D_MODEL),
            k_prompt, v_prompt, stack(outs_p, 0), stack(outs_p, 1),
            stack(outs_s, 0), stack(outs_s, 1), stack(outs_s, 2), stack(outs_s, 3))
```

```python
import functools

import jax
import jax.numpy as jnp
from jax import lax
from jax.experimental import pallas as pl
from jax.experimental.pallas import tpu as pltpu

D_MODEL = 2048
HEAD_DIM = 128
CONV_DIM = 512
SB_DIM = 1024
RET_DIM = 512
SB_HEADS = 8
RET_HEADS = 4
CONV_WIDTH = 3
IN_COLS = 3 * CONV_DIM + 3 * SB_DIM + 4 * RET_DIM
FFN_DIM = 5632
PAGE_SIZE = 128
ROPE_BASE = 10000.0
RMS_EPS = 1e-6
SCALE = HEAD_DIM ** -0.5

COL_CB, COL_CC, COL_CH = 0, 4, 8
COL_SQ, COL_SK, COL_SV = 12, 20, 28
COL_RQ, COL_RK, COL_RV, COL_RG = 36, 40, 44, 48

VMEM_LIMIT_BYTES = 56 * 1024 * 1024
BF16 = jnp.bfloat16
F32 = jnp.float32


def _params(*semantics):
    return pltpu.CompilerParams(dimension_semantics=semantics, vmem_limit_bytes=VMEM_LIMIT_BYTES)


def _rms(x, gain):
    return x * lax.rsqrt(jnp.mean(x * x, axis=-1, keepdims=True) + RMS_EPS) * gain


def _softplus(z):
    return jnp.maximum(z, 0.0) + jnp.log(1.0 + jnp.exp(-jnp.abs(z)))


def _dot(a, b):
    return jnp.dot(a, b, preferred_element_type=F32)


def _dot_nt(a, b):
    return lax.dot_general(a, b, (((1,), (1,)), ((), ())), preferred_element_type=F32)


def _suffix_matrix():
    r = lax.broadcasted_iota(jnp.int32, (2 * HEAD_DIM, 2 * HEAD_DIM), 0) % HEAD_DIM
    c = lax.broadcasted_iota(jnp.int32, (2 * HEAD_DIM, 2 * HEAD_DIM), 1)
    return jnp.where((r > c) | (c >= HEAD_DIM), -1.0, 0.0).astype(BF16)


def _suffix_and_total(soft, tri):
    hi = soft.astype(BF16)
    lo = (soft - hi.astype(F32)).astype(BF16)
    s = _dot(jnp.concatenate([hi, lo], axis=1), tri)
    return s[:, :HEAD_DIM], s[:, HEAD_DIM:]


def _norm_cast_kernel(x_ref, g_ref, o_ref):
    o_ref[...] = _rms(x_ref[...], g_ref[...]).astype(BF16)


def norm_cast(x, gain, tm):
    m = x.shape[0]
    return pl.pallas_call(
        _norm_cast_kernel,
        grid=(m // tm,),
        in_specs=[pl.BlockSpec((tm, D_MODEL), lambda i: (i, 0)),
                  pl.BlockSpec((1, D_MODEL), lambda i: (0, 0))],
        out_specs=pl.BlockSpec((tm, D_MODEL), lambda i: (i, 0)),
        out_shape=jax.ShapeDtypeStruct((m, D_MODEL), BF16),
        compiler_params=_params("parallel"),
        name="norm_cast",
    )(x, gain.reshape(1, D_MODEL))


def _in_proj_kernel(a_ref, a2_ref, w_ref, o_ref, o2_ref, wb_ref):
    @pl.when(pl.program_id(1) == 0)
    def _():
        wb_ref[...] = w_ref[...].astype(BF16)
        o2_ref[...] = _dot(a2_ref[...], wb_ref[...])

    o_ref[...] = _dot(a_ref[...], wb_ref[...])


def in_proj(a, a2, w, layer, tm, tn):
    m, k = a.shape
    m2 = a2.shape[0]
    n = w.shape[2]
    return pl.pallas_call(
        _in_proj_kernel,
        grid=(n // tn, m // tm),
        in_specs=[pl.BlockSpec((tm, k), lambda j, i: (i, 0)),
                  pl.BlockSpec((m2, k), lambda j, i: (0, 0)),
                  pl.BlockSpec((None, k, tn), lambda j, i: (layer, 0, j))],
        out_specs=[pl.BlockSpec((tm, tn), lambda j, i: (i, j)),
                   pl.BlockSpec((m2, tn), lambda j, i: (0, j))],
        out_shape=[jax.ShapeDtypeStruct((m, n), F32), jax.ShapeDtypeStruct((m2, n), F32)],
        scratch_shapes=[pltpu.VMEM((k, tn), BF16)],
        compiler_params=_params("arbitrary", "arbitrary"),
        name="in_proj",
    )(a, a2, w)


def _out_proj_kernel(c_ref, s_ref, r_ref, w_ref, x_ref, gpost_ref, gnext_ref, x_out_ref, xn_out_ref):
    mix = jnp.concatenate([c_ref[...], s_ref[...], r_ref[...]], axis=-1)
    x = x_ref[...] + _rms(_dot(mix, w_ref[...]), gpost_ref[...])
    x_out_ref[...] = x
    xn_out_ref[...] = _rms(x, gnext_ref[...]).astype(BF16)


def out_proj(conv_out, sb_out, ret_out, w, layer, x, g_post, g_next, tm):
    m = x.shape[0]
    row = lambda i: (i, 0)
    fixed = lambda i: (0, 0)
    return pl.pallas_call(
        _out_proj_kernel,
        grid=(m // tm,),
        in_specs=[pl.BlockSpec((tm, CONV_DIM), row),
                  pl.BlockSpec((tm, SB_DIM), row),
                  pl.BlockSpec((tm, RET_DIM), row),
                  pl.BlockSpec((None, D_MODEL, D_MODEL), lambda i: (layer, 0, 0)),
                  pl.BlockSpec((tm, D_MODEL), row),
                  pl.BlockSpec((1, D_MODEL), fixed),
                  pl.BlockSpec((1, D_MODEL), fixed)],
        out_specs=[pl.BlockSpec((tm, D_MODEL), row), pl.BlockSpec((tm, D_MODEL), row)],
        out_shape=[jax.ShapeDtypeStruct((m, D_MODEL), F32), jax.ShapeDtypeStruct((m, D_MODEL), BF16)],
        compiler_params=_params("parallel"),
        name="out_proj",
    )(conv_out, sb_out, ret_out, w, x, g_post.reshape(1, -1), g_next.reshape(1, -1))


def _gate_up_kernel(a_ref, a2_ref, wg_ref, wu_ref, o_ref, o2_ref, wgb_ref, wub_ref):
    def swiglu(a):
        gate = _dot(a, wgb_ref[...])
        up = _dot(a, wub_ref[...])
        return (gate * jax.nn.sigmoid(gate) * up).astype(BF16)

    @pl.when(pl.program_id(1) == 0)
    def _():
        wgb_ref[...] = wg_ref[...].astype(BF16)
        wub_ref[...] = wu_ref[...].astype(BF16)
        o2_ref[...] = swiglu(a2_ref[...])

    o_ref[...] = swiglu(a_ref[...])


def gate_up(a, a2, w, layer, tm, tf):
    m, m2 = a.shape[0], a2.shape[0]
    nf = FFN_DIM // tf
    return pl.pallas_call(
        _gate_up_kernel,
        grid=(nf, m // tm),
        in_specs=[pl.BlockSpec((tm, D_MODEL), lambda j, i: (i, 0)),
                  pl.BlockSpec((m2, D_MODEL), lambda j, i: (0, 0)),
                  pl.BlockSpec((None, D_MODEL, tf), lambda j, i: (layer, 0, j)),
                  pl.BlockSpec((None, D_MODEL, tf), lambda j, i: (layer, 0, j + nf))],
        out_specs=[pl.BlockSpec((tm, tf), lambda j, i: (i, j)),
                   pl.BlockSpec((m2, tf), lambda j, i: (0, j))],
        out_shape=[jax.ShapeDtypeStruct((m, FFN_DIM), BF16), jax.ShapeDtypeStruct((m2, FFN_DIM), BF16)],
        scratch_shapes=[pltpu.VMEM((D_MODEL, tf), BF16), pltpu.VMEM((D_MODEL, tf), BF16)],
        compiler_params=_params("arbitrary", "arbitrary"),
        name="gate_up",
    )(a, a2, w, w)


def _down_kernel(emit_next, h_ref, w_ref, x_ref, gpost_ref, *rest):
    x = x_ref[...] + _rms(_dot(h_ref[...], w_ref[...]), gpost_ref[...])
    if emit_next:
        gnext_ref, x_out_ref, xn_out_ref = rest
        xn_out_ref[...] = _rms(x, gnext_ref[...]).astype(BF16)
    else:
        x_out_ref, = rest
    x_out_ref[...] = x


def down_proj(h, w, layer, x, g_post, g_next, tm):
    m = x.shape[0]
    emit_next = g_next is not None
    row = lambda i: (i, 0)
    fixed = lambda i: (0, 0)
    in_specs = [pl.BlockSpec((tm, FFN_DIM), row),
                pl.BlockSpec((None, FFN_DIM, D_MODEL), lambda i: (layer, 0, 0), pipeline_mode=pl.Buffered(1)),
                pl.BlockSpec((tm, D_MODEL), row),
                pl.BlockSpec((1, D_MODEL), fixed)]
    args = [h, w, x, g_post.reshape(1, -1)]
    out_specs = [pl.BlockSpec((tm, D_MODEL), row)]
    out_shape = [jax.ShapeDtypeStruct((m, D_MODEL), F32)]
    if emit_next:
        in_specs.append(pl.BlockSpec((1, D_MODEL), fixed))
        args.append(g_next.reshape(1, -1))
        out_specs.append(pl.BlockSpec((tm, D_MODEL), row))
        out_shape.append(jax.ShapeDtypeStruct((m, D_MODEL), BF16))
    outs = pl.pallas_call(
        functools.partial(_down_kernel, emit_next),
        grid=(m // tm,),
        in_specs=in_specs,
        out_specs=out_specs,
        out_shape=out_shape,
        compiler_params=_params("parallel"),
        name="down_proj",
    )(*args)
    return (outs[0], outs[1]) if emit_next else (outs[0], None)


def _conv_prompt_kernel(cb_ref, cc_ref, ch_ref, st_ref, w_ref, o_ref, new_ref):
    u = cc_ref[0] * ch_ref[0]
    seq = u.shape[0]
    st = st_ref[0]
    w = w_ref[...]
    row = lax.broadcasted_iota(jnp.int32, u.shape, 0)
    back1 = jnp.where(row == 0, st[1:2], pltpu.roll(u, 1, 0))
    back2 = jnp.where(row == 0, st[0:1], jnp.where(row == 1, st[1:2], pltpu.roll(u, 2, 0)))
    y = back2 * w[0:1] + back1 * w[1:2] + u * w[2:3]
    o_ref[0] = (cb_ref[0] * y).astype(BF16)
    new_ref[0] = u[seq - 2:, :]


def conv_prompt(proj3, state, conv_w):
    b, seq, _ = proj3.shape
    col = lambda off: pl.BlockSpec((1, seq, HEAD_DIM), lambda i, c: (i, 0, off + c))
    return pl.pallas_call(
        _conv_prompt_kernel,
        grid=(b, CONV_DIM // HEAD_DIM),
        in_specs=[col(COL_CB), col(COL_CC), col(COL_CH),
                  pl.BlockSpec((1, CONV_WIDTH - 1, HEAD_DIM), lambda i, c: (i, 0, c)),
                  pl.BlockSpec((CONV_WIDTH, HEAD_DIM), lambda i, c: (0, c))],
        out_specs=[pl.BlockSpec((1, seq, HEAD_DIM), lambda i, c: (i, 0, c)),
                   pl.BlockSpec((1, CONV_WIDTH - 1, HEAD_DIM), lambda i, c: (i, 0, c))],
        out_shape=[jax.ShapeDtypeStruct((b, seq, CONV_DIM), BF16),
                   jax.ShapeDtypeStruct((b, CONV_WIDTH - 1, CONV_DIM), F32)],
        compiler_params=_params("parallel", "parallel"),
        name="conv_prompt",
    )(proj3, proj3, proj3, state, conv_w)


def _conv_step_kernel(cb_ref, cc_ref, ch_ref, s0_ref, s1_ref, w_ref, o_ref, u_ref):
    u = cc_ref[...] * ch_ref[...]
    w = w_ref[...]
    y = s0_ref[...] * w[0:1] + s1_ref[...] * w[1:2] + u * w[2:3]
    o_ref[...] = (cb_ref[...] * y).astype(BF16)
    u_ref[...] = u


def conv_step(proj, state, conv_w):
    n = proj.shape[0]
    col = lambda j: pl.BlockSpec((n, CONV_DIM), lambda i: (0, j))
    full = pl.BlockSpec((n, CONV_DIM), lambda i: (0, 0))
    out, u = pl.pallas_call(
        _conv_step_kernel,
        grid=(1,),
        in_specs=[col(0), col(1), col(2), full, full,
                  pl.BlockSpec((CONV_WIDTH, CONV_DIM), lambda i: (0, 0))],
        out_specs=[full, full],
        out_shape=[jax.ShapeDtypeStruct((n, CONV_DIM), BF16), jax.ShapeDtypeStruct((n, CONV_DIM), F32)],
        compiler_params=_params("arbitrary"),
        name="conv_step",
    )(proj, proj, proj, state[:, 0], state[:, 1], conv_w)
    return out, jnp.stack([state[:, 1], u], axis=1)


SB_QUERY_BLOCK = 4 * HEAD_DIM


def _sb_prompt_kernel(q_ref, k_ref, v_ref, bias_ref, gain_ref, *rest):
    o_ref, k_out_ref, v_out_ref = rest[-3:]
    tq = SB_QUERY_BLOCK
    k_out_ref[...] = k_ref[...]
    v_out_ref[...] = v_ref[...]

    n_blk = tq // HEAD_DIM
    n_str = 2
    rows = tq // n_str
    stream = lambda i: slice(i * rows, (i + 1) * rows)
    keys = lambda ref, g: ref[0, g * tq:(g + 1) * tq, :].astype(BF16)

    def weights(zs, laters, bias, tri, masked):
        stacks, betas, masks = [], [], []
        for i in range(n_str):
            z = zs[i] * SCALE + bias[:, :zs[i].shape[1]]
            soft = _softplus(z)
            betas.append(z - soft)
            if masked:
                mask = (lax.broadcasted_iota(jnp.int32, z.shape, 1)
                        < lax.broadcasted_iota(jnp.int32, z.shape, 0) + i * rows)
                soft = jnp.where(mask, soft, 0.0)
                masks.append(mask)
            blocks = [soft[:, j * HEAD_DIM:(j + 1) * HEAD_DIM] for j in range(z.shape[1] // HEAD_DIM)]
            stacks.append(jnp.concatenate(blocks, axis=0))
        sums = [_suffix_and_total(stacks[i], tri) for i in range(n_str)]
        ws, new_laters = [], []
        for i in range(n_str):
            suffix, total = sums[i]
            later = laters[i]
            parts = [None] * (zs[i].shape[1] // HEAD_DIM)
            for j in reversed(range(len(parts))):
                parts[j] = suffix[j * rows:(j + 1) * rows] + later
                later = later + total[j * rows:(j + 1) * rows]
            a = jnp.exp(betas[i] + jnp.concatenate(parts, axis=1))
            if masked:
                a = jnp.where(masks[i], a, 0.0)
            ws.append(a.astype(BF16))
            new_laters.append(later)
        return ws, new_laters

    tri = _suffix_matrix()
    bias = jnp.concatenate([bias_ref[0]] * n_blk, axis=1)

    def query_block(diag):
        q = q_ref[0, diag * tq:(diag + 1) * tq, :].astype(BF16)
        n_keys = lambda g, i: (i + 1) * rows if g == diag else tq
        logits = lambda g: [_dot_nt(q[stream(i)], keys(k_ref, g)[:n_keys(g, i)]) for i in range(n_str)]
        accs = [jnp.zeros((rows, HEAD_DIM), F32) for _ in range(n_str)]
        laters = list(accs)
        zs, ws = logits(diag), None
        for g in range(diag, -1, -1):
            if ws is not None:
                vg = keys(v_ref, g + 1)
                accs = [accs[i] + _dot(ws[i], vg[:n_keys(g + 1, i)]) for i in range(n_str)]
            zs_next = logits(g - 1) if g > 0 else None
            ws, laters = weights(zs, laters, bias, tri, g == diag)
            zs = zs_next
        v0 = keys(v_ref, 0)
        acc = jnp.concatenate([accs[i] + _dot(ws[i], v0[:n_keys(0, i)]) for i in range(n_str)], axis=0)
        o_ref[0, diag * tq:(diag + 1) * tq, :] = (_rms(acc, 1.0) * gain_ref[0]).astype(BF16)

    for diag in range(k_ref.shape[1] // tq):
        query_block(diag)


def sb_prompt(proj3, bias, gain, layer, depth, kv_all):
    b, seq, _ = proj3.shape
    assert seq % SB_QUERY_BLOCK == 0
    col = lambda off: pl.BlockSpec((1, seq, HEAD_DIM), lambda i, h: (i, 0, off + h))
    per_head = pl.BlockSpec((1, 1, HEAD_DIM), lambda i, h: (h, 0, 0))
    kv_out = pl.BlockSpec((None, 1, seq, HEAD_DIM), lambda i, h: (layer, i, 0, h))
    kv_shape = jax.ShapeDtypeStruct((depth, b, seq, SB_DIM), F32)
    in_specs = [col(COL_SQ), col(COL_SK), col(COL_SV), per_head, per_head]
    args = [proj3, proj3, proj3,
            jnp.broadcast_to(bias[:, None, None], (SB_HEADS, 1, HEAD_DIM)),
            gain.reshape(SB_HEADS, 1, HEAD_DIM)]
    aliases = {}
    if kv_all is not None:
        aliases = {len(args): 1, len(args) + 1: 2}
        in_specs += [pl.BlockSpec(memory_space=pl.ANY)] * 2
        args += list(kv_all)
    sb_out, k_all, v_all = pl.pallas_call(
        _sb_prompt_kernel,
        grid=(b, SB_HEADS),
        in_specs=in_specs,
        out_specs=[pl.BlockSpec((1, seq, HEAD_DIM), lambda i, h: (i, 0, h)), kv_out, kv_out],
        out_shape=[jax.ShapeDtypeStruct((b, seq, SB_DIM), BF16), kv_shape, kv_shape],
        input_output_aliases=aliases,
        compiler_params=_params("parallel", "parallel"),
        name="sb_prompt",
    )(*args)
    return sb_out, (k_all, v_all)


DECODE_PAGES_PER_STEP = 16


def _sb_decode_kernel(n_groups, pt_ref, q_ref, *refs):
    del pt_ref
    n_pg = DECODE_PAGES_PER_STEP
    k_refs, v_refs = refs[:n_pg], refs[n_pg:2 * n_pg]
    bias_ref, gain_ref, o_ref, acc_ref, later_ref, a_ref = refs[2 * n_pg:]
    s = pl.program_id(0)
    last = pl.num_programs(0) - 1
    group_k = jnp.minimum(s, last - 1) % n_groups
    group_v = jnp.maximum(s - 1, 0) % n_groups

    @pl.when(s == 0)
    def _():
        acc_ref[...] = jnp.zeros_like(acc_ref)
        later_ref[...] = jnp.zeros_like(later_ref)
        a_ref[...] = jnp.zeros_like(a_ref)

    head = lax.broadcasted_iota(jnp.int32, (SB_HEADS, HEAD_DIM), 0)
    only = lambda x, h: jnp.where(head == h, x, 0.0).astype(BF16)
    head_rows = lambda ref, h: ref[0, 0, pl.ds(h, PAGE_SIZE, stride=SB_HEADS), :].astype(BF16)

    a_prev = a_ref[...]
    acc = jnp.where(group_v == 0, 0.0, acc_ref[...])
    for i, v_ref in enumerate(v_refs):
        a_page = a_prev[i * SB_HEADS:(i + 1) * SB_HEADS]
        out = _dot(only(a_page, 0), head_rows(v_ref, 0))
        for h in range(1, SB_HEADS):
            out = out + _dot(only(a_page, h), head_rows(v_ref, h))
        acc = acc + out
    acc_ref[...] = acc

    q = q_ref[0]
    q_rows = [only(q, h) for h in range(SB_HEADS)]
    zs = []
    for k_ref in k_refs:
        z = _dot_nt(q_rows[0], head_rows(k_ref, 0))
        for h in range(1, SB_HEADS):
            z = z + _dot_nt(q_rows[h], head_rows(k_ref, h))
        zs.append(z)
    bias = jnp.concatenate([bias_ref[...]] * n_pg, axis=0)
    z2 = jnp.concatenate(zs, axis=0) * SCALE + bias
    soft = _softplus(z2)
    suffix, total = _suffix_and_total(soft, _suffix_matrix())
    later = jnp.where(group_k == 0, 0.0, later_ref[...])
    parts = []
    for i in range(n_pg):
        parts.append(suffix[i * SB_HEADS:(i + 1) * SB_HEADS] + later)
        later = later + total[i * SB_HEADS:(i + 1) * SB_HEADS]
    later_ref[...] = later
    a_ref[...] = jnp.exp(z2 - soft + jnp.concatenate(parts, axis=0))

    @pl.when((s > 0) & (group_v == n_groups - 1))
    def _():
        o_ref[0] = _rms(acc_ref[...], 1.0) * gain_ref[...]


def sb_decode(q, cache_k, cache_v, layer, page_table, bias, gain):
    n, n_pages = page_table.shape
    n_pg = DECODE_PAGES_PER_STEP
    n_groups, rem = divmod(n_pages, n_pg)
    assert rem == 0
    n_steps = n * n_groups
    rows = PAGE_SIZE * SB_HEADS
    depth, n_pool = cache_k.shape[:2]
    ck = cache_k.reshape(depth, n_pool, rows, HEAD_DIM)
    cv = cache_v.reshape(depth, n_pool, rows, HEAD_DIM)
    step_k = lambda s: jnp.minimum(s, n_steps - 1)
    step_v = lambda s: jnp.maximum(s - 1, 0)

    def page(step, i_pg):
        def index(s, pt):
            t = step(s)
            return layer, pt[t // n_groups, n_pages - 1 - (n_pg * (t % n_groups) + i_pg)], 0, 0
        return pl.BlockSpec((1, 1, rows, HEAD_DIM), index)

    per_head = pl.BlockSpec((SB_HEADS, HEAD_DIM), lambda s, pt: (0, 0))
    seq_block = lambda step: pl.BlockSpec((1, SB_HEADS, HEAD_DIM), lambda s, pt: (step(s) // n_groups, 0, 0))
    state = pltpu.VMEM((SB_HEADS, HEAD_DIM), F32)
    return pl.pallas_call(
        functools.partial(_sb_decode_kernel, n_groups),
        grid_spec=pltpu.PrefetchScalarGridSpec(
            num_scalar_prefetch=1,
            grid=(n_steps + 1,),
            in_specs=([seq_block(step_k)] + [page(step_k, i) for i in range(n_pg)]
                      + [page(step_v, i) for i in range(n_pg)] + [per_head, per_head]),
            out_specs=seq_block(step_v),
            scratch_shapes=[state, state, pltpu.VMEM((n_pg * SB_HEADS, HEAD_DIM), F32)]),
        out_shape=jax.ShapeDtypeStruct((n, SB_HEADS, HEAD_DIM), F32),
        compiler_params=_params("arbitrary"),
        name="sb_decode",
    )(page_table, q, *([ck] * n_pg), *([cv] * n_pg),
      jnp.broadcast_to(bias[:, None], (SB_HEADS, HEAD_DIM)),
      gain.reshape(SB_HEADS, HEAD_DIM))


def _rotate(x, cos2, sin2):
    return x * cos2 + pltpu.roll(x, HEAD_DIM // 2, 1) * sin2


def _retention_kernel(q_ref, k_ref, v_ref, g_ref, cos_ref, sin_ref, lg_ref, s_ref, o_ref, s_out_ref):
    chunk = float(HEAD_DIM)
    n_chunks = q_ref.shape[1] // HEAD_DIM
    lg = lg_ref[0]
    r = lax.broadcasted_iota(jnp.int32, (HEAD_DIM, HEAD_DIM), 0).astype(F32)
    c = lax.broadcasted_iota(jnp.int32, (HEAD_DIM, HEAD_DIM), 1).astype(F32)
    diff = r - c
    dmat = jnp.where(diff >= 0, jnp.exp(jnp.maximum(diff, 0.0) * lg), 0.0)
    cross = jnp.exp((r + 1.0) * lg)
    k_decay = jnp.exp((chunk - 1.0 - r) * lg)
    s_decay = jnp.exp(chunk * lg)

    def body(i, s):
        rows = slice(i * HEAD_DIM, (i + 1) * HEAD_DIM)
        cos2 = cos_ref[rows, :]
        sin2 = sin_ref[rows, :]
        q = _rotate(q_ref[0, rows, :], cos2, sin2).astype(BF16)
        k = _rotate(k_ref[0, rows, :], cos2, sin2) * SCALE
        v = v_ref[0, rows, :].astype(BF16)
        inner = _dot_nt(q, k.astype(BF16)) * dmat
        o = _dot(inner.astype(BF16), v) + _dot(q, s.astype(BF16)) * cross
        g = g_ref[0, rows, :]
        o_ref[0, rows, :] = (g * jax.nn.sigmoid(g) * _rms(o, 1.0)).astype(o_ref.dtype)
        return s_decay * s + _dot((k * k_decay).T.astype(BF16), v)

    s = s_ref[0, 0]
    for i in range(n_chunks):
        s = body(i, s)
    s_out_ref[0, 0] = s


def retention(proj3, cos2, sin2, log_gamma, state):
    b, rows, _ = proj3.shape
    col = lambda off: pl.BlockSpec((1, rows, HEAD_DIM), lambda i, h: (i, 0, off + h))
    table = pl.BlockSpec((rows, HEAD_DIM), lambda i, h: (0, 0))
    st = pl.BlockSpec((1, 1, HEAD_DIM, HEAD_DIM), lambda i, h: (i, h, 0, 0))
    return pl.pallas_call(
        _retention_kernel,
        grid=(b, RET_HEADS),
        in_specs=[col(COL_RQ), col(COL_RK), col(COL_RV), col(COL_RG), table, table,
                  pl.BlockSpec((1, 1, HEAD_DIM), lambda i, h: (h, 0, 0)), st],
        out_specs=[pl.BlockSpec((1, rows, HEAD_DIM), lambda i, h: (i, 0, h)), st],
        out_shape=[jax.ShapeDtypeStruct((b, rows, RET_DIM), BF16),
                   jax.ShapeDtypeStruct((b, RET_HEADS, HEAD_DIM, HEAD_DIM), F32)],
        compiler_params=_params("parallel", "parallel"),
        name="retention",
    )(proj3, proj3, proj3, proj3, cos2, sin2,
      jnp.broadcast_to(log_gamma[:, None, None], (RET_HEADS, 1, HEAD_DIM)), state)


RET_STEP_ROWS = 8


def _retention_step_kernel(q_ref, k_ref, v_ref, g_ref, cos_ref, sin_ref, lg_ref, s_ref, o_ref, s_out_ref):
    n_rows = q_ref.shape[0]
    cos2, sin2 = cos_ref[...], sin_ref[...]
    fill = jnp.zeros((HEAD_DIM - n_rows, HEAD_DIM), F32)
    columns = lambda x: jnp.concatenate([x, fill], axis=0).T
    for h in range(RET_HEADS):
        lanes = slice(h * HEAD_DIM, (h + 1) * HEAD_DIM)
        gamma = jnp.exp(lg_ref[h])
        q = _rotate(q_ref[:, lanes], cos2, sin2)
        k = _rotate(k_ref[:, lanes], cos2, sin2) * SCALE
        v = v_ref[:, lanes]
        inner = jnp.sum(q * k, axis=-1, keepdims=True)
        q_cols, k_cols = columns(q), columns(k)
        outs = []
        for r in range(n_rows):
            state = s_ref[r, h]
            v_row = v[r:r + 1]
            q_state = jnp.sum(q_cols[:, r:r + 1] * state, axis=0, keepdims=True)
            outs.append(inner[r:r + 1] * v_row + q_state * gamma)
            s_out_ref[r, h] = gamma * state + k_cols[:, r:r + 1] * v_row
        g = g_ref[:, lanes]
        o_ref[:, lanes] = g * jax.nn.sigmoid(g) * _rms(jnp.concatenate(outs, axis=0), 1.0)


def retention_step(proj, cos2, sin2, log_gamma, state):
    n = proj.shape[0]
    rows = RET_STEP_ROWS
    col = lambda j: pl.BlockSpec((rows, RET_DIM), lambda i: (i, COL_RQ * HEAD_DIM // RET_DIM + j))
    table = pl.BlockSpec((1, HEAD_DIM), lambda i: (0, 0))
    st = pl.BlockSpec((rows, RET_HEADS, HEAD_DIM, HEAD_DIM), lambda i: (i, 0, 0, 0))
    return pl.pallas_call(
        _retention_step_kernel,
        grid=(n // rows,),
        in_specs=[col(0), col(1), col(2), col(3), table, table,
                  pl.BlockSpec((RET_HEADS, 1, HEAD_DIM), lambda i: (0, 0, 0)), st],
        out_specs=[pl.BlockSpec((rows, RET_DIM), lambda i: (i, 0)), st],
        out_shape=[jax.ShapeDtypeStruct((n, RET_DIM), F32),
                   jax.ShapeDtypeStruct((n, RET_HEADS, HEAD_DIM, HEAD_DIM), F32)],
        compiler_params=_params("parallel"),
        name="retention_step",
    )(proj, proj, proj, proj, cos2, sin2,
      jnp.broadcast_to(log_gamma[:, None, None], (RET_HEADS, 1, HEAD_DIM)), state)


def _rotary_tables(pos):
    inv = ROPE_BASE ** (-jnp.arange(0, HEAD_DIM, 2, dtype=F32) / HEAD_DIM)
    ang = pos.astype(F32)[:, None] * inv[None, :]
    cos, sin = jnp.cos(ang), jnp.sin(ang)
    return jnp.concatenate([cos, cos], axis=-1), jnp.concatenate([-sin, sin], axis=-1)


def _tiles(m):
    return {"tm": min(m, 512), "tm_wide": min(m, 1024), "tm_down": min(m, 256), "tn_in": IN_COLS // 4, "tf": 512}


def _log_gamma():
    return jnp.log1p(-jnp.exp2(-5.0 - jnp.arange(RET_HEADS, dtype=F32)))


def _mix_prompt(proj, b, seq, p, layer, depth, kv_all):
    proj3 = proj.reshape(b, seq, IN_COLS)
    conv_out, conv_new = conv_prompt(proj3, jnp.zeros((b, CONV_WIDTH - 1, CONV_DIM), F32), p["conv_w"])
    sb_out, kv_all = sb_prompt(proj3, p["sb_bias"], p["sb_gain"], layer, depth, kv_all)
    cos2, sin2 = _rotary_tables(jnp.arange(seq, dtype=jnp.int32))
    ret_out, ret_new = retention(proj3, cos2, sin2, _log_gamma(),
                                 jnp.zeros((b, RET_HEADS, HEAD_DIM, HEAD_DIM), F32))
    mix = (conv_out.reshape(b * seq, CONV_DIM), sb_out.reshape(b * seq, SB_DIM), ret_out.reshape(b * seq, RET_DIM))
    return mix, kv_all, conv_new, ret_new


def _mix_sample(proj, p, cache_k, cache_v, layer, state_conv, state_ret, page_table):
    m = proj.shape[0]
    conv_out, conv_new = conv_step(proj, state_conv, p["conv_w"])
    heads = lambda col: proj[:, col * HEAD_DIM:(col + SB_HEADS) * HEAD_DIM].reshape(m, SB_HEADS, HEAD_DIM)
    sb_out = sb_decode(heads(COL_SQ), cache_k, cache_v, layer, page_table, p["sb_bias"], p["sb_gain"])
    past_len = page_table.shape[1] * PAGE_SIZE
    cos2, sin2 = _rotary_tables(jnp.full((1,), past_len, jnp.int32))
    ret_out, ret_new = retention_step(proj, cos2, sin2, _log_gamma(), state_ret)
    mix = (conv_out, sb_out.reshape(m, SB_DIM).astype(BF16), ret_out.astype(BF16))
    return mix, heads(COL_SK)[:, None], heads(COL_SV)[:, None], conv_new, ret_new


def _mix_out(mix, x, p, layer):
    return out_proj(*mix, p["w_out"], layer, x, p["g_mix_post"], p["g_ffn_pre"], _tiles(x.shape[0])["tm"])


def _ffn_out(h, x, p, layer, g_next):
    return down_proj(h, p["w_down"], layer, x, p["g_ffn_post"], g_next, _tiles(x.shape[0])["tm_down"])


def kernel(x_prompt, x_sample, cache_k, cache_v, state_conv, state_ret, page_table, norm_mix_pre, norm_mix_post, norm_ffn_pre, norm_ffn_post, w_in, conv_w, sb_bias, sb_gain, w_out, w_gate_up, w_down):
    b, seq, _ = x_prompt.shape
    n = x_sample.shape[0]
    depth = w_in.shape[0]
    xp = x_prompt.reshape(b * seq, D_MODEL)
    xs = x_sample.reshape(n, D_MODEL)
    xpn = norm_cast(xp, norm_mix_pre[0], _tiles(b * seq)["tm"])
    xsn = norm_cast(xs, norm_mix_pre[0], _tiles(n)["tm"])
    kv_all = None
    outs_p, outs_s = [], []
    w_out_bf, w_down_bf = w_out.astype(BF16), w_down.astype(BF16)
    t = _tiles(b * seq)
    for l in range(depth):
        p = {"w_out": w_out_bf, "w_down": w_down_bf,
             "conv_w": conv_w[l], "sb_bias": sb_bias[l], "sb_gain": sb_gain[l],
             "g_mix_post": norm_mix_post[l], "g_ffn_pre": norm_ffn_pre[l], "g_ffn_post": norm_ffn_post[l]}
        g_next = norm_mix_pre[l + 1] if l + 1 < depth else None
        proj_p, proj_s = in_proj(xpn, xsn, w_in, l, t["tm"], t["tn_in"])
        mix_p, kv_all, *rest_p = _mix_prompt(proj_p, b, seq, p, l, depth, kv_all)
        mix_s, *rest_s = _mix_sample(proj_s, p, cache_k, cache_v, l, state_conv[l], state_ret[l], page_table)
        xp, xpn = _mix_out(mix_p, xp, p, l)
        xs, xsn = _mix_out(mix_s, xs, p, l)
        hp, hs = gate_up(xpn, xsn, w_gate_up, l, t["tm_wide"], t["tf"])
        xp, xpn = _ffn_out(hp, xp, p, l, g_next)
        xs, xsn = _ffn_out(hs, xs, p, l, g_next)
        outs_p.append(rest_p)
        outs_s.append(rest_s)
    stack = lambda outs, i: jnp.stack([o[i] for o in outs])
    k_prompt, v_prompt = (a.reshape(depth, b, seq, SB_HEADS, HEAD_DIM) for a in kv_all)
    return (xp.reshape(b, seq, D_MODEL), xs.reshape(n, 1, D_MODEL),
            k_prompt, v_prompt, stack(outs_p, 0), stack(outs_p, 1),
            stack(outs_s, 0), stack(outs_s, 1), stack(outs_s, 2), stack(outs_s, 3))
```

```python
import functools

import jax
import jax.numpy as jnp
from jax import lax
from jax.experimental import pallas as pl
from jax.experimental.pallas import tpu as pltpu

D_MODEL = 2048
HEAD_DIM = 128
CONV_DIM = 512
SB_DIM = 1024
RET_DIM = 512
SB_HEADS = 8
RET_HEADS = 4
CONV_WIDTH = 3
IN_COLS = 3 * CONV_DIM + 3 * SB_DIM + 4 * RET_DIM
FFN_DIM = 5632
PAGE_SIZE = 128
ROPE_BASE = 10000.0
RMS_EPS = 1e-6
SCALE = HEAD_DIM ** -0.5

COL_CB, COL_CC, COL_CH = 0, 4, 8
COL_SQ, COL_SK, COL_SV = 12, 20, 28
COL_RQ, COL_RK, COL_RV, COL_RG = 36, 40, 44, 48

VMEM_LIMIT_BYTES = 56 * 1024 * 1024
BF16 = jnp.bfloat16
F32 = jnp.float32


def _params(*semantics):
    return pltpu.CompilerParams(dimension_semantics=semantics, vmem_limit_bytes=VMEM_LIMIT_BYTES)


def _rms(x, gain):
    return x * lax.rsqrt(jnp.mean(x * x, axis=-1, keepdims=True) + RMS_EPS) * gain


def _softplus(z):
    return jnp.maximum(z, 0.0) + jnp.log(1.0 + jnp.exp(-jnp.abs(z)))


def _dot(a, b):
    return jnp.dot(a, b, preferred_element_type=F32)


def _dot_nt(a, b):
    return lax.dot_general(a, b, (((1,), (1,)), ((), ())), preferred_element_type=F32)


def _suffix_matrix():
    r = lax.broadcasted_iota(jnp.int32, (2 * HEAD_DIM, 2 * HEAD_DIM), 0) % HEAD_DIM
    c = lax.broadcasted_iota(jnp.int32, (2 * HEAD_DIM, 2 * HEAD_DIM), 1)
    return jnp.where((r > c) | (c >= HEAD_DIM), -1.0, 0.0).astype(BF16)


def _suffix_and_total(soft, tri):
    hi = soft.astype(BF16)
    lo = (soft - hi.astype(F32)).astype(BF16)
    s = _dot(jnp.concatenate([hi, lo], axis=1), tri)
    return s[:, :HEAD_DIM], s[:, HEAD_DIM:]


def _norm_cast_kernel(x_ref, g_ref, o_ref):
    o_ref[...] = _rms(x_ref[...], g_ref[...]).astype(BF16)


def norm_cast(x, gain, tm):
    m = x.shape[0]
    return pl.pallas_call(
        _norm_cast_kernel,
        grid=(m // tm,),
        in_specs=[pl.BlockSpec((tm, D_MODEL), lambda i: (i, 0)),
                  pl.BlockSpec((1, D_MODEL), lambda i: (0, 0))],
        out_specs=pl.BlockSpec((tm, D_MODEL), lambda i: (i, 0)),
        out_shape=jax.ShapeDtypeStruct((m, D_MODEL), BF16),
        compiler_params=_params("parallel"),
        name="norm_cast",
    )(x, gain.reshape(1, D_MODEL))


def _in_proj_kernel(a_ref, a2_ref, w_ref, o_ref, o2_ref, wb_ref):
    @pl.when(pl.program_id(1) == 0)
    def _():
        wb_ref[...] = w_ref[...].astype(BF16)
        o2_ref[...] = _dot(a2_ref[...], wb_ref[...])

    o_ref[...] = _dot(a_ref[...], wb_ref[...])


def in_proj(a, a2, w, layer, tm, tn):
    m, k = a.shape
    m2 = a2.shape[0]
    n = w.shape[2]
    return pl.pallas_call(
        _in_proj_kernel,
        grid=(n // tn, m // tm),
        in_specs=[pl.BlockSpec((tm, k), lambda j, i: (i, 0)),
                  pl.BlockSpec((m2, k), lambda j, i: (0, 0)),
                  pl.BlockSpec((None, k, tn), lambda j, i: (layer, 0, j))],
        out_specs=[pl.BlockSpec((tm, tn), lambda j, i: (i, j)),
                   pl.BlockSpec((m2, tn), lambda j, i: (0, j))],
        out_shape=[jax.ShapeDtypeStruct((m, n), F32), jax.ShapeDtypeStruct((m2, n), F32)],
        scratch_shapes=[pltpu.VMEM((k, tn), BF16)],
        compiler_params=_params("arbitrary", "arbitrary"),
        name="in_proj",
    )(a, a2, w)


def _out_proj_kernel(c_ref, s_ref, r_ref, x_ref, c2_ref, s2_ref, r2_ref, x2_ref, w_ref, gpost_ref, gnext_ref,
                     x_out_ref, xn_out_ref, x2_out_ref, xn2_out_ref):
    def project(c, s, r):
        return _dot(jnp.concatenate([c, s, r], axis=-1), w_ref[...])

    def finish(y, x):
        x = x + _rms(y, gpost_ref[...])
        return x, _rms(x, gnext_ref[...]).astype(BF16)

    @pl.when(pl.program_id(0) == 0)
    def _():
        x2_out_ref[...], xn2_out_ref[...] = finish(project(c2_ref[...], s2_ref[...], r2_ref[...]), x2_ref[...])

    half = c_ref.shape[0] // 2
    halves = [slice(0, half), slice(half, 2 * half)]
    ys = [project(c_ref[rows, :], s_ref[rows, :], r_ref[rows, :]) for rows in halves]
    for rows, y in zip(halves, ys):
        x_out_ref[rows, :], xn_out_ref[rows, :] = finish(y, x_ref[rows, :])


def out_proj(mix, x, mix2, x2, w, layer, g_post, g_next, tm):
    m, m2 = x.shape[0], x2.shape[0]
    row = lambda i: (i, 0)
    fixed = lambda i: (0, 0)
    widths = (CONV_DIM, SB_DIM, RET_DIM, D_MODEL)
    return pl.pallas_call(
        _out_proj_kernel,
        grid=(m // tm,),
        in_specs=([pl.BlockSpec((tm, n), row) for n in widths] + [pl.BlockSpec((m2, n), fixed) for n in widths]
                  + [pl.BlockSpec((None, D_MODEL, D_MODEL), lambda i: (layer, 0, 0)),
                     pl.BlockSpec((1, D_MODEL), fixed), pl.BlockSpec((1, D_MODEL), fixed)]),
        out_specs=[pl.BlockSpec((tm, D_MODEL), row), pl.BlockSpec((tm, D_MODEL), row),
                   pl.BlockSpec((m2, D_MODEL), fixed), pl.BlockSpec((m2, D_MODEL), fixed)],
        out_shape=[jax.ShapeDtypeStruct((m, D_MODEL), F32), jax.ShapeDtypeStruct((m, D_MODEL), BF16),
                   jax.ShapeDtypeStruct((m2, D_MODEL), F32), jax.ShapeDtypeStruct((m2, D_MODEL), BF16)],
        compiler_params=_params("arbitrary"),
        name="out_proj",
    )(*mix, x, *mix2, x2, w, g_post.reshape(1, -1), g_next.reshape(1, -1))


def _gate_up_kernel(a_ref, a2_ref, wg_ref, wu_ref, o_ref, o2_ref, wgb_ref, wub_ref):
    def swiglu(a):
        gate = _dot(a, wgb_ref[...])
        up = _dot(a, wub_ref[...])
        return (gate * jax.nn.sigmoid(gate) * up).astype(BF16)

    @pl.when(pl.program_id(1) == 0)
    def _():
        wgb_ref[...] = wg_ref[...].astype(BF16)
        wub_ref[...] = wu_ref[...].astype(BF16)
        o2_ref[...] = swiglu(a2_ref[...])

    o_ref[...] = swiglu(a_ref[...])


def gate_up(a, a2, w, layer, tm, tf):
    m, m2 = a.shape[0], a2.shape[0]
    nf = FFN_DIM // tf
    return pl.pallas_call(
        _gate_up_kernel,
        grid=(nf, m // tm),
        in_specs=[pl.BlockSpec((tm, D_MODEL), lambda j, i: (i, 0)),
                  pl.BlockSpec((m2, D_MODEL), lambda j, i: (0, 0)),
                  pl.BlockSpec((None, D_MODEL, tf), lambda j, i: (layer, 0, j)),
                  pl.BlockSpec((None, D_MODEL, tf), lambda j, i: (layer, 0, j + nf))],
        out_specs=[pl.BlockSpec((tm, tf), lambda j, i: (i, j)),
                   pl.BlockSpec((m2, tf), lambda j, i: (0, j))],
        out_shape=[jax.ShapeDtypeStruct((m, FFN_DIM), BF16), jax.ShapeDtypeStruct((m2, FFN_DIM), BF16)],
        scratch_shapes=[pltpu.VMEM((D_MODEL, tf), BF16), pltpu.VMEM((D_MODEL, tf), BF16)],
        compiler_params=_params("arbitrary", "arbitrary"),
        name="gate_up",
    )(a, a2, w, w)


def _down_kernel(emit_next, h_ref, x_ref, h2_ref, x2_ref, w_ref, gpost_ref, *rest):
    if emit_next:
        gnext_ref, x_out_ref, x2_out_ref, xn_out_ref, xn2_out_ref = rest
    else:
        x_out_ref, x2_out_ref = rest

    def finish(h, x, out_ref, norm_ref):
        x = x + _rms(_dot(h, w_ref[...]), gpost_ref[...])
        out_ref[...] = x
        if emit_next:
            norm_ref[...] = _rms(x, gnext_ref[...]).astype(BF16)

    @pl.when(pl.program_id(0) == 0)
    def _():
        finish(h2_ref[...], x2_ref[...], x2_out_ref, xn2_out_ref if emit_next else None)

    finish(h_ref[...], x_ref[...], x_out_ref, xn_out_ref if emit_next else None)


def down_proj(h, x, h2, x2, w, layer, g_post, g_next, tm):
    m, m2 = x.shape[0], x2.shape[0]
    emit_next = g_next is not None
    row = lambda i: (i, 0)
    fixed = lambda i: (0, 0)
    in_specs = [pl.BlockSpec((tm, FFN_DIM), row), pl.BlockSpec((tm, D_MODEL), row),
                pl.BlockSpec((m2, FFN_DIM), fixed), pl.BlockSpec((m2, D_MODEL), fixed),
                pl.BlockSpec((None, FFN_DIM, D_MODEL), lambda i: (layer, 0, 0), pipeline_mode=pl.Buffered(1)),
                pl.BlockSpec((1, D_MODEL), fixed)]
    args = [h, x, h2, x2, w, g_post.reshape(1, -1)]
    out_specs = [pl.BlockSpec((tm, D_MODEL), row), pl.BlockSpec((m2, D_MODEL), fixed)]
    out_shape = [jax.ShapeDtypeStruct((m, D_MODEL), F32), jax.ShapeDtypeStruct((m2, D_MODEL), F32)]
    if emit_next:
        in_specs.append(pl.BlockSpec((1, D_MODEL), fixed))
        args.append(g_next.reshape(1, -1))
        out_specs += [pl.BlockSpec((tm, D_MODEL), row), pl.BlockSpec((m2, D_MODEL), fixed)]
        out_shape += [jax.ShapeDtypeStruct((m, D_MODEL), BF16), jax.ShapeDtypeStruct((m2, D_MODEL), BF16)]
    outs = pl.pallas_call(
        functools.partial(_down_kernel, emit_next),
        grid=(m // tm,),
        in_specs=in_specs,
        out_specs=out_specs,
        out_shape=out_shape,
        compiler_params=_params("arbitrary"),
        name="down_proj",
    )(*args)
    return tuple(outs) if emit_next else (outs[0], outs[1], None, None)


def _conv_prompt_kernel(cb_ref, cc_ref, ch_ref, st_ref, w_ref, o_ref, new_ref):
    u = cc_ref[0] * ch_ref[0]
    seq = u.shape[0]
    st = st_ref[0]
    w = w_ref[...]
    row = lax.broadcasted_iota(jnp.int32, u.shape, 0)
    back1 = jnp.where(row == 0, st[1:2], pltpu.roll(u, 1, 0))
    back2 = jnp.where(row == 0, st[0:1], jnp.where(row == 1, st[1:2], pltpu.roll(u, 2, 0)))
    y = back2 * w[0:1] + back1 * w[1:2] + u * w[2:3]
    o_ref[0] = (cb_ref[0] * y).astype(BF16)
    new_ref[0] = u[seq - 2:, :]


def conv_prompt(proj3, state, conv_w):
    b, seq, _ = proj3.shape
    col = lambda off: pl.BlockSpec((1, seq, HEAD_DIM), lambda i, c: (i, 0, off + c))
    return pl.pallas_call(
        _conv_prompt_kernel,
        grid=(b, CONV_DIM // HEAD_DIM),
        in_specs=[col(COL_CB), col(COL_CC), col(COL_CH),
                  pl.BlockSpec((1, CONV_WIDTH - 1, HEAD_DIM), lambda i, c: (i, 0, c)),
                  pl.BlockSpec((CONV_WIDTH, HEAD_DIM), lambda i, c: (0, c))],
        out_specs=[pl.BlockSpec((1, seq, HEAD_DIM), lambda i, c: (i, 0, c)),
                   pl.BlockSpec((1, CONV_WIDTH - 1, HEAD_DIM), lambda i, c: (i, 0, c))],
        out_shape=[jax.ShapeDtypeStruct((b, seq, CONV_DIM), BF16),
                   jax.ShapeDtypeStruct((b, CONV_WIDTH - 1, CONV_DIM), F32)],
        compiler_params=_params("parallel", "parallel"),
        name="conv_prompt",
    )(proj3, proj3, proj3, state, conv_w)


def _conv_step_kernel(cb_ref, cc_ref, ch_ref, s0_ref, s1_ref, w_ref, o_ref, u_ref):
    u = cc_ref[...] * ch_ref[...]
    w = w_ref[...]
    y = s0_ref[...] * w[0:1] + s1_ref[...] * w[1:2] + u * w[2:3]
    o_ref[...] = (cb_ref[...] * y).astype(BF16)
    u_ref[...] = u


def conv_step(proj, state, conv_w):
    n = proj.shape[0]
    col = lambda j: pl.BlockSpec((n, CONV_DIM), lambda i: (0, j))
    full = pl.BlockSpec((n, CONV_DIM), lambda i: (0, 0))
    out, u = pl.pallas_call(
        _conv_step_kernel,
        grid=(1,),
        in_specs=[col(0), col(1), col(2), full, full,
                  pl.BlockSpec((CONV_WIDTH, CONV_DIM), lambda i: (0, 0))],
        out_specs=[full, full],
        out_shape=[jax.ShapeDtypeStruct((n, CONV_DIM), BF16), jax.ShapeDtypeStruct((n, CONV_DIM), F32)],
        compiler_params=_params("arbitrary"),
        name="conv_step",
    )(proj, proj, proj, state[:, 0], state[:, 1], conv_w)
    return out, jnp.stack([state[:, 1], u], axis=1)


SB_QUERY_BLOCK = 4 * HEAD_DIM


def _sb_prompt_kernel(q_ref, k_ref, v_ref, bias_ref, gain_ref, *rest):
    o_ref, k_out_ref, v_out_ref = rest[-3:]
    tq = SB_QUERY_BLOCK
    k_out_ref[...] = k_ref[...]
    v_out_ref[...] = v_ref[...]

    n_blk = tq // HEAD_DIM
    n_str = 2
    rows = tq // n_str
    stream = lambda i: slice(i * rows, (i + 1) * rows)
    keys = lambda ref, g: ref[0, g * tq:(g + 1) * tq, :].astype(BF16)

    def weights(zs, laters, bias, tri, masked):
        stacks, betas, masks = [], [], []
        for i in range(n_str):
            z = zs[i] * SCALE + bias[:, :zs[i].shape[1]]
            soft = _softplus(z)
            betas.append(z - soft)
            if masked:
                mask = (lax.broadcasted_iota(jnp.int32, z.shape, 1)
                        < lax.broadcasted_iota(jnp.int32, z.shape, 0) + i * rows)
                soft = jnp.where(mask, soft, 0.0)
                masks.append(mask)
            blocks = [soft[:, j * HEAD_DIM:(j + 1) * HEAD_DIM] for j in range(z.shape[1] // HEAD_DIM)]
            stacks.append(jnp.concatenate(blocks, axis=0))
        sums = [_suffix_and_total(stacks[i], tri) for i in range(n_str)]
        ws, new_laters = [], []
        for i in range(n_str):
            suffix, total = sums[i]
            later = laters[i]
            parts = [None] * (zs[i].shape[1] // HEAD_DIM)
            for j in reversed(range(len(parts))):
                parts[j] = suffix[j * rows:(j + 1) * rows] + later
                later = later + total[j * rows:(j + 1) * rows]
            a = jnp.exp(betas[i] + jnp.concatenate(parts, axis=1))
            if masked:
                a = jnp.where(masks[i], a, 0.0)
            ws.append(a.astype(BF16))
            new_laters.append(later)
        return ws, new_laters

    tri = _suffix_matrix()
    bias = jnp.concatenate([bias_ref[0]] * n_blk, axis=1)

    def query_block(diag):
        q = q_ref[0, diag * tq:(diag + 1) * tq, :].astype(BF16)
        n_keys = lambda g, i: (i + 1) * rows if g == diag else tq
        logits = lambda g: [_dot_nt(q[stream(i)], keys(k_ref, g)[:n_keys(g, i)]) for i in range(n_str)]
        accs = [jnp.zeros((rows, HEAD_DIM), F32) for _ in range(n_str)]
        laters = list(accs)
        zs, ws = logits(diag), None
        for g in range(diag, -1, -1):
            if ws is not None:
                vg = keys(v_ref, g + 1)
                accs = [accs[i] + _dot(ws[i], vg[:n_keys(g + 1, i)]) for i in range(n_str)]
            zs_next = logits(g - 1) if g > 0 else None
            ws, laters = weights(zs, laters, bias, tri, g == diag)
            zs = zs_next
        v0 = keys(v_ref, 0)
        acc = jnp.concatenate([accs[i] + _dot(ws[i], v0[:n_keys(0, i)]) for i in range(n_str)], axis=0)
        o_ref[0, diag * tq:(diag + 1) * tq, :] = (_rms(acc, 1.0) * gain_ref[0]).astype(BF16)

    for diag in range(k_ref.shape[1] // tq):
        query_block(diag)


def sb_prompt(proj3, bias, gain, layer, depth, kv_all):
    b, seq, _ = proj3.shape
    assert seq % SB_QUERY_BLOCK == 0
    col = lambda off: pl.BlockSpec((1, seq, HEAD_DIM), lambda i, h: (i, 0, off + h))
    per_head = pl.BlockSpec((1, 1, HEAD_DIM), lambda i, h: (h, 0, 0))
    kv_out = pl.BlockSpec((None, 1, seq, HEAD_DIM), lambda i, h: (layer, i, 0, h))
    kv_shape = jax.ShapeDtypeStruct((depth, b, seq, SB_DIM), F32)
    in_specs = [col(COL_SQ), col(COL_SK), col(COL_SV), per_head, per_head]
    args = [proj3, proj3, proj3,
            jnp.broadcast_to(bias[:, None, None], (SB_HEADS, 1, HEAD_DIM)),
            gain.reshape(SB_HEADS, 1, HEAD_DIM)]
    aliases = {}
    if kv_all is not None:
        aliases = {len(args): 1, len(args) + 1: 2}
        in_specs += [pl.BlockSpec(memory_space=pl.ANY)] * 2
        args += list(kv_all)
    sb_out, k_all, v_all = pl.pallas_call(
        _sb_prompt_kernel,
        grid=(b, SB_HEADS),
        in_specs=in_specs,
        out_specs=[pl.BlockSpec((1, seq, HEAD_DIM), lambda i, h: (i, 0, h)), kv_out, kv_out],
        out_shape=[jax.ShapeDtypeStruct((b, seq, SB_DIM), BF16), kv_shape, kv_shape],
        input_output_aliases=aliases,
        compiler_params=_params("parallel", "parallel"),
        name="sb_prompt",
    )(*args)
    return sb_out, (k_all, v_all)


DECODE_PAGES_PER_STEP = 16


def _sb_decode_kernel(n_groups, pt_ref, q_ref, *refs):
    del pt_ref
    n_pg = DECODE_PAGES_PER_STEP
    k_refs, v_refs = refs[:n_pg], refs[n_pg:2 * n_pg]
    bias_ref, gain_ref, o_ref, acc_ref, later_ref, a_ref = refs[2 * n_pg:]
    s = pl.program_id(0)
    last = pl.num_programs(0) - 1
    group_k = jnp.minimum(s, last - 1) % n_groups
    group_v = jnp.maximum(s - 1, 0) % n_groups

    @pl.when(s == 0)
    def _():
        acc_ref[...] = jnp.zeros_like(acc_ref)
        later_ref[...] = jnp.zeros_like(later_ref)
        a_ref[...] = jnp.zeros_like(a_ref)

    head = lax.broadcasted_iota(jnp.int32, (SB_HEADS, HEAD_DIM), 0)
    only = lambda x, h: jnp.where(head == h, x, 0.0).astype(BF16)
    head_rows = lambda ref, h: ref[0, 0, pl.ds(h, PAGE_SIZE, stride=SB_HEADS), :].astype(BF16)

    a_prev = a_ref[...]
    acc = jnp.where(group_v == 0, 0.0, acc_ref[...])
    for i, v_ref in enumerate(v_refs):
        a_page = a_prev[i * SB_HEADS:(i + 1) * SB_HEADS]
        out = _dot(only(a_page, 0), head_rows(v_ref, 0))
        for h in range(1, SB_HEADS):
            out = out + _dot(only(a_page, h), head_rows(v_ref, h))
        acc = acc + out
    acc_ref[...] = acc

    q = q_ref[0]
    q_rows = [only(q, h) for h in range(SB_HEADS)]
    zs = []
    for k_ref in k_refs:
        z = _dot_nt(q_rows[0], head_rows(k_ref, 0))
        for h in range(1, SB_HEADS):
            z = z + _dot_nt(q_rows[h], head_rows(k_ref, h))
        zs.append(z)
    bias = jnp.concatenate([bias_ref[...]] * n_pg, axis=0)
    z2 = jnp.concatenate(zs, axis=0) * SCALE + bias
    soft = _softplus(z2)
    suffix, total = _suffix_and_total(soft, _suffix_matrix())
    later = jnp.where(group_k == 0, 0.0, later_ref[...])
    parts = []
    for i in range(n_pg):
        parts.append(suffix[i * SB_HEADS:(i + 1) * SB_HEADS] + later)
        later = later + total[i * SB_HEADS:(i + 1) * SB_HEADS]
    later_ref[...] = later
    a_ref[...] = jnp.exp(z2 - soft + jnp.concatenate(parts, axis=0))

    @pl.when((s > 0) & (group_v == n_groups - 1))
    def _():
        o_ref[0] = _rms(acc_ref[...], 1.0) * gain_ref[...]


def sb_decode(q, cache_k, cache_v, layer, page_table, bias, gain):
    n, n_pages = page_table.shape
    n_pg = DECODE_PAGES_PER_STEP
    n_groups, rem = divmod(n_pages, n_pg)
    assert rem == 0
    n_steps = n * n_groups
    rows = PAGE_SIZE * SB_HEADS
    depth, n_pool = cache_k.shape[:2]
    ck = cache_k.reshape(depth, n_pool, rows, HEAD_DIM)
    cv = cache_v.reshape(depth, n_pool, rows, HEAD_DIM)
    step_k = lambda s: jnp.minimum(s, n_steps - 1)
    step_v = lambda s: jnp.maximum(s - 1, 0)

    def page(step, i_pg):
        def index(s, pt):
            t = step(s)
            return layer, pt[t // n_groups, n_pages - 1 - (n_pg * (t % n_groups) + i_pg)], 0, 0
        return pl.BlockSpec((1, 1, rows, HEAD_DIM), index)

    per_head = pl.BlockSpec((SB_HEADS, HEAD_DIM), lambda s, pt: (0, 0))
    seq_block = lambda step: pl.BlockSpec((1, SB_HEADS, HEAD_DIM), lambda s, pt: (step(s) // n_groups, 0, 0))
    state = pltpu.VMEM((SB_HEADS, HEAD_DIM), F32)
    return pl.pallas_call(
        functools.partial(_sb_decode_kernel, n_groups),
        grid_spec=pltpu.PrefetchScalarGridSpec(
            num_scalar_prefetch=1,
            grid=(n_steps + 1,),
            in_specs=([seq_block(step_k)] + [page(step_k, i) for i in range(n_pg)]
                      + [page(step_v, i) for i in range(n_pg)] + [per_head, per_head]),
            out_specs=seq_block(step_v),
            scratch_shapes=[state, state, pltpu.VMEM((n_pg * SB_HEADS, HEAD_DIM), F32)]),
        out_shape=jax.ShapeDtypeStruct((n, SB_HEADS, HEAD_DIM), F32),
        compiler_params=_params("arbitrary"),
        name="sb_decode",
    )(page_table, q, *([ck] * n_pg), *([cv] * n_pg),
      jnp.broadcast_to(bias[:, None], (SB_HEADS, HEAD_DIM)),
      gain.reshape(SB_HEADS, HEAD_DIM))


def _rotate(x, cos2, sin2):
    return x * cos2 + pltpu.roll(x, HEAD_DIM // 2, 1) * sin2


def _retention_kernel(q_ref, k_ref, v_ref, g_ref, cos_ref, sin_ref, lg_ref, s_ref, o_ref, s_out_ref):
    chunk = float(HEAD_DIM)
    n_chunks = q_ref.shape[1] // HEAD_DIM
    lg = lg_ref[0]
    r = lax.broadcasted_iota(jnp.int32, (HEAD_DIM, HEAD_DIM), 0).astype(F32)
    c = lax.broadcasted_iota(jnp.int32, (HEAD_DIM, HEAD_DIM), 1).astype(F32)
    diff = r - c
    dmat = jnp.where(diff >= 0, jnp.exp(jnp.maximum(diff, 0.0) * lg), 0.0)
    cross = jnp.exp((r + 1.0) * lg)
    k_decay = jnp.exp((chunk - 1.0 - r) * lg)
    s_decay = jnp.exp(chunk * lg)

    def body(i, s):
        rows = slice(i * HEAD_DIM, (i + 1) * HEAD_DIM)
        cos2 = cos_ref[rows, :]
        sin2 = sin_ref[rows, :]
        q = _rotate(q_ref[0, rows, :], cos2, sin2).astype(BF16)
        k = _rotate(k_ref[0, rows, :], cos2, sin2) * SCALE
        v = v_ref[0, rows, :].astype(BF16)
        inner = _dot_nt(q, k.astype(BF16)) * dmat
        o = _dot(inner.astype(BF16), v) + _dot(q, s.astype(BF16)) * cross
        g = g_ref[0, rows, :]
        o_ref[0, rows, :] = (g * jax.nn.sigmoid(g) * _rms(o, 1.0)).astype(o_ref.dtype)
        return s_decay * s + _dot((k * k_decay).T.astype(BF16), v)

    s = s_ref[0, 0]
    for i in range(n_chunks):
        s = body(i, s)
    s_out_ref[0, 0] = s


def retention(proj3, cos2, sin2, log_gamma, state):
    b, rows, _ = proj3.shape
    col = lambda off: pl.BlockSpec((1, rows, HEAD_DIM), lambda i, h: (i, 0, off + h))
    table = pl.BlockSpec((rows, HEAD_DIM), lambda i, h: (0, 0))
    st = pl.BlockSpec((1, 1, HEAD_DIM, HEAD_DIM), lambda i, h: (i, h, 0, 0))
    return pl.pallas_call(
        _retention_kernel,
        grid=(b, RET_HEADS),
        in_specs=[col(COL_RQ), col(COL_RK), col(COL_RV), col(COL_RG), table, table,
                  pl.BlockSpec((1, 1, HEAD_DIM), lambda i, h: (h, 0, 0)), st],
        out_specs=[pl.BlockSpec((1, rows, HEAD_DIM), lambda i, h: (i, 0, h)), st],
        out_shape=[jax.ShapeDtypeStruct((b, rows, RET_DIM), BF16),
                   jax.ShapeDtypeStruct((b, RET_HEADS, HEAD_DIM, HEAD_DIM), F32)],
        compiler_params=_params("parallel", "parallel"),
        name="retention",
    )(proj3, proj3, proj3, proj3, cos2, sin2,
      jnp.broadcast_to(log_gamma[:, None, None], (RET_HEADS, 1, HEAD_DIM)), state)


RET_STEP_ROWS = 8


def _retention_step_kernel(q_ref, k_ref, v_ref, g_ref, cos_ref, sin_ref, lg_ref, s_ref, o_ref, s_out_ref):
    n_rows = q_ref.shape[0]
    cos2, sin2 = cos_ref[...], sin_ref[...]
    fill = jnp.zeros((HEAD_DIM - n_rows, HEAD_DIM), F32)
    columns = lambda x: jnp.concatenate([x, fill], axis=0).T
    for h in range(RET_HEADS):
        lanes = slice(h * HEAD_DIM, (h + 1) * HEAD_DIM)
        gamma = jnp.exp(lg_ref[h])
        q = _rotate(q_ref[:, lanes], cos2, sin2)
        k = _rotate(k_ref[:, lanes], cos2, sin2) * SCALE
        v = v_ref[:, lanes]
        inner = jnp.sum(q * k, axis=-1, keepdims=True)
        q_cols, k_cols = columns(q), columns(k)
        outs = []
        for r in range(n_rows):
            state = s_ref[r, h]
            v_row = v[r:r + 1]
            q_state = jnp.sum(q_cols[:, r:r + 1] * state, axis=0, keepdims=True)
            outs.append(inner[r:r + 1] * v_row + q_state * gamma)
            s_out_ref[r, h] = gamma * state + k_cols[:, r:r + 1] * v_row
        g = g_ref[:, lanes]
        o_ref[:, lanes] = g * jax.nn.sigmoid(g) * _rms(jnp.concatenate(outs, axis=0), 1.0)


def retention_step(proj, cos2, sin2, log_gamma, state):
    n = proj.shape[0]
    rows = RET_STEP_ROWS
    col = lambda j: pl.BlockSpec((rows, RET_DIM), lambda i: (i, COL_RQ * HEAD_DIM // RET_DIM + j))
    table = pl.BlockSpec((1, HEAD_DIM), lambda i: (0, 0))
    st = pl.BlockSpec((rows, RET_HEADS, HEAD_DIM, HEAD_DIM), lambda i: (i, 0, 0, 0))
    return pl.pallas_call(
        _retention_step_kernel,
        grid=(n // rows,),
        in_specs=[col(0), col(1), col(2), col(3), table, table,
                  pl.BlockSpec((RET_HEADS, 1, HEAD_DIM), lambda i: (0, 0, 0)), st],
        out_specs=[pl.BlockSpec((rows, RET_DIM), lambda i: (i, 0)), st],
        out_shape=[jax.ShapeDtypeStruct((n, RET_DIM), F32),
                   jax.ShapeDtypeStruct((n, RET_HEADS, HEAD_DIM, HEAD_DIM), F32)],
        compiler_params=_params("parallel"),
        name="retention_step",
    )(proj, proj, proj, proj, cos2, sin2,
      jnp.broadcast_to(log_gamma[:, None, None], (RET_HEADS, 1, HEAD_DIM)), state)


def _rotary_tables(pos):
    inv = ROPE_BASE ** (-jnp.arange(0, HEAD_DIM, 2, dtype=F32) / HEAD_DIM)
    ang = pos.astype(F32)[:, None] * inv[None, :]
    cos, sin = jnp.cos(ang), jnp.sin(ang)
    return jnp.concatenate([cos, cos], axis=-1), jnp.concatenate([-sin, sin], axis=-1)


def _tiles(m):
    return {"tm": min(m, 512), "tm_wide": min(m, 2048), "tm_down": min(m, 256), "tn_in": IN_COLS // 4, "tf": 512}


def _log_gamma():
    return jnp.log1p(-jnp.exp2(-5.0 - jnp.arange(RET_HEADS, dtype=F32)))


def _mix_prompt(proj, b, seq, p, layer, depth, kv_all):
    proj3 = proj.reshape(b, seq, IN_COLS)
    conv_out, conv_new = conv_prompt(proj3, jnp.zeros((b, CONV_WIDTH - 1, CONV_DIM), F32), p["conv_w"])
    sb_out, kv_all = sb_prompt(proj3, p["sb_bias"], p["sb_gain"], layer, depth, kv_all)
    cos2, sin2 = _rotary_tables(jnp.arange(seq, dtype=jnp.int32))
    ret_out, ret_new = retention(proj3, cos2, sin2, _log_gamma(),
                                 jnp.zeros((b, RET_HEADS, HEAD_DIM, HEAD_DIM), F32))
    mix = (conv_out.reshape(b * seq, CONV_DIM), sb_out.reshape(b * seq, SB_DIM), ret_out.reshape(b * seq, RET_DIM))
    return mix, kv_all, conv_new, ret_new


def _mix_sample(proj, p, cache_k, cache_v, layer, state_conv, state_ret, page_table):
    m = proj.shape[0]
    conv_out, conv_new = conv_step(proj, state_conv, p["conv_w"])
    heads = lambda col: proj[:, col * HEAD_DIM:(col + SB_HEADS) * HEAD_DIM].reshape(m, SB_HEADS, HEAD_DIM)
    sb_out = sb_decode(heads(COL_SQ), cache_k, cache_v, layer, page_table, p["sb_bias"], p["sb_gain"])
    past_len = page_table.shape[1] * PAGE_SIZE
    cos2, sin2 = _rotary_tables(jnp.full((1,), past_len, jnp.int32))
    ret_out, ret_new = retention_step(proj, cos2, sin2, _log_gamma(), state_ret)
    mix = (conv_out, sb_out.reshape(m, SB_DIM).astype(BF16), ret_out.astype(BF16))
    return mix, heads(COL_SK)[:, None], heads(COL_SV)[:, None], conv_new, ret_new


def kernel(x_prompt, x_sample, cache_k, cache_v, state_conv, state_ret, page_table, norm_mix_pre, norm_mix_post, norm_ffn_pre, norm_ffn_post, w_in, conv_w, sb_bias, sb_gain, w_out, w_gate_up, w_down):
    b, seq, _ = x_prompt.shape
    n = x_sample.shape[0]
    depth = w_in.shape[0]
    xp = x_prompt.reshape(b * seq, D_MODEL)
    xs = x_sample.reshape(n, D_MODEL)
    xpn = norm_cast(xp, norm_mix_pre[0], _tiles(b * seq)["tm"])
    xsn = norm_cast(xs, norm_mix_pre[0], _tiles(n)["tm"])
    kv_all = None
    outs_p, outs_s = [], []
    w_out_bf, w_down_bf = w_out.astype(BF16), w_down.astype(BF16)
    t = _tiles(b * seq)
    for l in range(depth):
        p = {"conv_w": conv_w[l], "sb_bias": sb_bias[l], "sb_gain": sb_gain[l],
             "g_mix_post": norm_mix_post[l], "g_ffn_pre": norm_ffn_pre[l], "g_ffn_post": norm_ffn_post[l]}
        g_next = norm_mix_pre[l + 1] if l + 1 < depth else None
        proj_p, proj_s = in_proj(xpn, xsn, w_in, l, t["tm"], t["tn_in"])
        mix_p, kv_all, *rest_p = _mix_prompt(proj_p, b, seq, p, l, depth, kv_all)
        mix_s, *rest_s = _mix_sample(proj_s, p, cache_k, cache_v, l, state_conv[l], state_ret[l], page_table)
        xp, xpn, xs, xsn = out_proj(mix_p, xp, mix_s, xs, w_out_bf, l, p["g_mix_post"], p["g_ffn_pre"], t["tm"])
        hp, hs = gate_up(xpn, xsn, w_gate_up, l, t["tm_wide"], t["tf"])
        xp, xs, xpn, xsn = down_proj(hp, xp, hs, xs, w_down_bf, l, p["g_ffn_post"], g_next, t["tm_down"])
        outs_p.append(rest_p)
        outs_s.append(rest_s)
    stack = lambda outs, i: jnp.stack([o[i] for o in outs])
    k_prompt, v_prompt = (a.reshape(depth, b, seq, SB_HEADS, HEAD_DIM) for a in kv_all)
    return (xp.reshape(b, seq, D_MODEL), xs.reshape(n, 1, D_MODEL),
            k_prompt, v_prompt, stack(outs_p, 0), stack(outs_p, 1),
            stack(outs_s, 0), stack(outs_s, 1), stack(outs_s, 2), stack(outs_s, 3))
```

```python
import functools

import jax
import jax.numpy as jnp
from jax import lax
from jax.experimental import pallas as pl
from jax.experimental.pallas import tpu as pltpu

D_MODEL = 2048
HEAD_DIM = 128
CONV_DIM = 512
SB_DIM = 1024
RET_DIM = 512
SB_HEADS = 8
RET_HEADS = 4
CONV_WIDTH = 3
IN_COLS = 3 * CONV_DIM + 3 * SB_DIM + 4 * RET_DIM
FFN_DIM = 5632
PAGE_SIZE = 128
ROPE_BASE = 10000.0
RMS_EPS = 1e-6
SCALE = HEAD_DIM ** -0.5

COL_CB, COL_CC, COL_CH = 0, 4, 8
COL_SQ, COL_SK, COL_SV = 12, 20, 28
COL_RQ, COL_RK, COL_RV, COL_RG = 36, 40, 44, 48

VMEM_LIMIT_BYTES = 56 * 1024 * 1024
BF16 = jnp.bfloat16
F32 = jnp.float32


def _params(*semantics):
    return pltpu.CompilerParams(dimension_semantics=semantics, vmem_limit_bytes=VMEM_LIMIT_BYTES)


def _rms(x, gain):
    return x * lax.rsqrt(jnp.mean(x * x, axis=-1, keepdims=True) + RMS_EPS) * gain


def _softplus(z):
    return jnp.maximum(z, 0.0) + jnp.log(1.0 + jnp.exp(-jnp.abs(z)))


def _dot(a, b):
    return jnp.dot(a, b, preferred_element_type=F32)


def _dot_nt(a, b):
    return lax.dot_general(a, b, (((1,), (1,)), ((), ())), preferred_element_type=F32)


def _suffix_matrix():
    r = lax.broadcasted_iota(jnp.int32, (2 * HEAD_DIM, 2 * HEAD_DIM), 0) % HEAD_DIM
    c = lax.broadcasted_iota(jnp.int32, (2 * HEAD_DIM, 2 * HEAD_DIM), 1)
    return jnp.where((r > c) | (c >= HEAD_DIM), -1.0, 0.0).astype(BF16)


def _suffix_and_total(soft, tri):
    hi = soft.astype(BF16)
    lo = (soft - hi.astype(F32)).astype(BF16)
    s = _dot(jnp.concatenate([hi, lo], axis=1), tri)
    return s[:, :HEAD_DIM], s[:, HEAD_DIM:]


def _norm_cast_kernel(x_ref, g_ref, o_ref):
    o_ref[...] = _rms(x_ref[...], g_ref[...]).astype(BF16)


def norm_cast(x, gain, tm):
    m = x.shape[0]
    return pl.pallas_call(
        _norm_cast_kernel,
        grid=(m // tm,),
        in_specs=[pl.BlockSpec((tm, D_MODEL), lambda i: (i, 0)),
                  pl.BlockSpec((1, D_MODEL), lambda i: (0, 0))],
        out_specs=pl.BlockSpec((tm, D_MODEL), lambda i: (i, 0)),
        out_shape=jax.ShapeDtypeStruct((m, D_MODEL), BF16),
        compiler_params=_params("parallel"),
        name="norm_cast",
    )(x, gain.reshape(1, D_MODEL))


def _in_proj_kernel(a_ref, a2_ref, w_ref, o_ref, o2_ref, wb_ref):
    @pl.when(pl.program_id(1) == 0)
    def _():
        wb_ref[...] = w_ref[...].astype(BF16)
        o2_ref[...] = _dot(a2_ref[...], wb_ref[...])

    o_ref[...] = _dot(a_ref[...], wb_ref[...])


def in_proj(a, a2, w, layer, tm, tn):
    m, k = a.shape
    m2 = a2.shape[0]
    n = w.shape[2]
    return pl.pallas_call(
        _in_proj_kernel,
        grid=(n // tn, m // tm),
        in_specs=[pl.BlockSpec((tm, k), lambda j, i: (i, 0)),
                  pl.BlockSpec((m2, k), lambda j, i: (0, 0)),
                  pl.BlockSpec((None, k, tn), lambda j, i: (layer, 0, j))],
        out_specs=[pl.BlockSpec((tm, tn), lambda j, i: (i, j)),
                   pl.BlockSpec((m2, tn), lambda j, i: (0, j))],
        out_shape=[jax.ShapeDtypeStruct((m, n), F32), jax.ShapeDtypeStruct((m2, n), F32)],
        scratch_shapes=[pltpu.VMEM((k, tn), BF16)],
        compiler_params=_params("arbitrary", "arbitrary"),
        name="in_proj",
    )(a, a2, w)


def _out_proj_kernel(c_ref, s_ref, r_ref, x_ref, c2_ref, s2_ref, r2_ref, x2_ref, w_ref, gpost_ref, gnext_ref,
                     x_out_ref, xn_out_ref, x2_out_ref, xn2_out_ref):
    def project(c, s, r):
        return _dot(jnp.concatenate([c, s, r], axis=-1), w_ref[...])

    def finish(y, x):
        x = x + _rms(y, gpost_ref[...])
        return x, _rms(x, gnext_ref[...]).astype(BF16)

    @pl.when(pl.program_id(0) == 0)
    def _():
        x2_out_ref[...], xn2_out_ref[...] = finish(project(c2_ref[...], s2_ref[...], r2_ref[...]), x2_ref[...])

    half = c_ref.shape[0] // 2
    halves = [slice(0, half), slice(half, 2 * half)]
    ys = [project(c_ref[rows, :], s_ref[rows, :], r_ref[rows, :]) for rows in halves]
    for rows, y in zip(halves, ys):
        x_out_ref[rows, :], xn_out_ref[rows, :] = finish(y, x_ref[rows, :])


def out_proj(mix, x, mix2, x2, w, layer, g_post, g_next, tm):
    m, m2 = x.shape[0], x2.shape[0]
    row = lambda i: (i, 0)
    fixed = lambda i: (0, 0)
    widths = (CONV_DIM, SB_DIM, RET_DIM, D_MODEL)
    return pl.pallas_call(
        _out_proj_kernel,
        grid=(m // tm,),
        in_specs=([pl.BlockSpec((tm, n), row) for n in widths] + [pl.BlockSpec((m2, n), fixed) for n in widths]
                  + [pl.BlockSpec((None, D_MODEL, D_MODEL), lambda i: (layer, 0, 0)),
                     pl.BlockSpec((1, D_MODEL), fixed), pl.BlockSpec((1, D_MODEL), fixed)]),
        out_specs=[pl.BlockSpec((tm, D_MODEL), row), pl.BlockSpec((tm, D_MODEL), row),
                   pl.BlockSpec((m2, D_MODEL), fixed), pl.BlockSpec((m2, D_MODEL), fixed)],
        out_shape=[jax.ShapeDtypeStruct((m, D_MODEL), F32), jax.ShapeDtypeStruct((m, D_MODEL), BF16),
                   jax.ShapeDtypeStruct((m2, D_MODEL), F32), jax.ShapeDtypeStruct((m2, D_MODEL), BF16)],
        compiler_params=_params("arbitrary"),
        name="out_proj",
    )(*mix, x, *mix2, x2, w, g_post.reshape(1, -1), g_next.reshape(1, -1))


def _gate_up_kernel(a_ref, a2_ref, wg_ref, wu_ref, o_ref, o2_ref, wgb_ref, wub_ref):
    def swiglu(a):
        gate = _dot(a, wgb_ref[...])
        up = _dot(a, wub_ref[...])
        return (gate * jax.nn.sigmoid(gate) * up).astype(BF16)

    @pl.when(pl.program_id(1) == 0)
    def _():
        wgb_ref[...] = wg_ref[...].astype(BF16)
        wub_ref[...] = wu_ref[...].astype(BF16)
        o2_ref[...] = swiglu(a2_ref[...])

    o_ref[...] = swiglu(a_ref[...])


def gate_up(a, a2, w, layer, tm, tf):
    m, m2 = a.shape[0], a2.shape[0]
    nf = FFN_DIM // tf
    return pl.pallas_call(
        _gate_up_kernel,
        grid=(nf, m // tm),
        in_specs=[pl.BlockSpec((tm, D_MODEL), lambda j, i: (i, 0)),
                  pl.BlockSpec((m2, D_MODEL), lambda j, i: (0, 0)),
                  pl.BlockSpec((None, D_MODEL, tf), lambda j, i: (layer, 0, j)),
                  pl.BlockSpec((None, D_MODEL, tf), lambda j, i: (layer, 0, j + nf))],
        out_specs=[pl.BlockSpec((tm, tf), lambda j, i: (i, j)),
                   pl.BlockSpec((m2, tf), lambda j, i: (0, j))],
        out_shape=[jax.ShapeDtypeStruct((m, FFN_DIM), BF16), jax.ShapeDtypeStruct((m2, FFN_DIM), BF16)],
        scratch_shapes=[pltpu.VMEM((D_MODEL, tf), BF16), pltpu.VMEM((D_MODEL, tf), BF16)],
        compiler_params=_params("arbitrary", "arbitrary"),
        name="gate_up",
    )(a, a2, w, w)


def _down_kernel(emit_next, h_ref, x_ref, h2_ref, x2_ref, w_ref, gpost_ref, *rest):
    if emit_next:
        gnext_ref, x_out_ref, x2_out_ref, xn_out_ref, xn2_out_ref = rest
    else:
        x_out_ref, x2_out_ref = rest

    def finish(h, x, out_ref, norm_ref):
        x = x + _rms(_dot(h, w_ref[...]), gpost_ref[...])
        out_ref[...] = x
        if emit_next:
            norm_ref[...] = _rms(x, gnext_ref[...]).astype(BF16)

    @pl.when(pl.program_id(0) == 0)
    def _():
        finish(h2_ref[...], x2_ref[...], x2_out_ref, xn2_out_ref if emit_next else None)

    finish(h_ref[...], x_ref[...], x_out_ref, xn_out_ref if emit_next else None)


def down_proj(h, x, h2, x2, w, layer, g_post, g_next, tm):
    m, m2 = x.shape[0], x2.shape[0]
    emit_next = g_next is not None
    row = lambda i: (i, 0)
    fixed = lambda i: (0, 0)
    in_specs = [pl.BlockSpec((tm, FFN_DIM), row), pl.BlockSpec((tm, D_MODEL), row),
                pl.BlockSpec((m2, FFN_DIM), fixed), pl.BlockSpec((m2, D_MODEL), fixed),
                pl.BlockSpec((None, FFN_DIM, D_MODEL), lambda i: (layer, 0, 0), pipeline_mode=pl.Buffered(1)),
                pl.BlockSpec((1, D_MODEL), fixed)]
    args = [h, x, h2, x2, w, g_post.reshape(1, -1)]
    out_specs = [pl.BlockSpec((tm, D_MODEL), row), pl.BlockSpec((m2, D_MODEL), fixed)]
    out_shape = [jax.ShapeDtypeStruct((m, D_MODEL), F32), jax.ShapeDtypeStruct((m2, D_MODEL), F32)]
    if emit_next:
        in_specs.append(pl.BlockSpec((1, D_MODEL), fixed))
        args.append(g_next.reshape(1, -1))
        out_specs += [pl.BlockSpec((tm, D_MODEL), row), pl.BlockSpec((m2, D_MODEL), fixed)]
        out_shape += [jax.ShapeDtypeStruct((m, D_MODEL), BF16), jax.ShapeDtypeStruct((m2, D_MODEL), BF16)]
    outs = pl.pallas_call(
        functools.partial(_down_kernel, emit_next),
        grid=(m // tm,),
        in_specs=in_specs,
        out_specs=out_specs,
        out_shape=out_shape,
        compiler_params=_params("arbitrary"),
        name="down_proj",
    )(*args)
    return tuple(outs) if emit_next else (outs[0], outs[1], None, None)


def _conv_prompt_kernel(cb_ref, cc_ref, ch_ref, st_ref, w_ref, o_ref, new_ref):
    u = cc_ref[0] * ch_ref[0]
    seq = u.shape[0]
    st = st_ref[0]
    w = w_ref[...]
    row = lax.broadcasted_iota(jnp.int32, u.shape, 0)
    back1 = jnp.where(row == 0, st[1:2], pltpu.roll(u, 1, 0))
    back2 = jnp.where(row == 0, st[0:1], jnp.where(row == 1, st[1:2], pltpu.roll(u, 2, 0)))
    y = back2 * w[0:1] + back1 * w[1:2] + u * w[2:3]
    o_ref[0] = (cb_ref[0] * y).astype(BF16)
    new_ref[0] = u[seq - 2:, :]


def conv_prompt(proj3, state, conv_w):
    b, seq, _ = proj3.shape
    col = lambda off: pl.BlockSpec((1, seq, HEAD_DIM), lambda i, c: (i, 0, off + c))
    return pl.pallas_call(
        _conv_prompt_kernel,
        grid=(b, CONV_DIM // HEAD_DIM),
        in_specs=[col(COL_CB), col(COL_CC), col(COL_CH),
                  pl.BlockSpec((1, CONV_WIDTH - 1, HEAD_DIM), lambda i, c: (i, 0, c)),
                  pl.BlockSpec((CONV_WIDTH, HEAD_DIM), lambda i, c: (0, c))],
        out_specs=[pl.BlockSpec((1, seq, HEAD_DIM), lambda i, c: (i, 0, c)),
                   pl.BlockSpec((1, CONV_WIDTH - 1, HEAD_DIM), lambda i, c: (i, 0, c))],
        out_shape=[jax.ShapeDtypeStruct((b, seq, CONV_DIM), BF16),
                   jax.ShapeDtypeStruct((b, CONV_WIDTH - 1, CONV_DIM), F32)],
        compiler_params=_params("parallel", "parallel"),
        name="conv_prompt",
    )(proj3, proj3, proj3, state, conv_w)


def _conv_step_kernel(cb_ref, cc_ref, ch_ref, s0_ref, s1_ref, w_ref, o_ref, u_ref):
    u = cc_ref[...] * ch_ref[...]
    w = w_ref[...]
    y = s0_ref[...] * w[0:1] + s1_ref[...] * w[1:2] + u * w[2:3]
    o_ref[...] = (cb_ref[...] * y).astype(BF16)
    u_ref[...] = u


def conv_step(proj, state, conv_w):
    n = proj.shape[0]
    col = lambda j: pl.BlockSpec((n, CONV_DIM), lambda i: (0, j))
    full = pl.BlockSpec((n, CONV_DIM), lambda i: (0, 0))
    out, u = pl.pallas_call(
        _conv_step_kernel,
        grid=(1,),
        in_specs=[col(0), col(1), col(2), full, full,
                  pl.BlockSpec((CONV_WIDTH, CONV_DIM), lambda i: (0, 0))],
        out_specs=[full, full],
        out_shape=[jax.ShapeDtypeStruct((n, CONV_DIM), BF16), jax.ShapeDtypeStruct((n, CONV_DIM), F32)],
        compiler_params=_params("arbitrary"),
        name="conv_step",
    )(proj, proj, proj, state[:, 0], state[:, 1], conv_w)
    return out, jnp.stack([state[:, 1], u], axis=1)


SB_QUERY_BLOCK = 4 * HEAD_DIM


def _sb_prompt_kernel(q_ref, k_ref, v_ref, bias_ref, gain_ref, *rest):
    o_ref, k_out_ref, v_out_ref = rest[-3:]
    tq = SB_QUERY_BLOCK
    head_rows = pl.ds(pl.program_id(1), k_ref.shape[1], stride=SB_HEADS)
    k_out_ref[0, head_rows, :] = k_ref[0]
    v_out_ref[0, head_rows, :] = v_ref[0]

    n_blk = tq // HEAD_DIM
    n_str = 2
    rows = tq // n_str
    stream = lambda i: slice(i * rows, (i + 1) * rows)
    keys = lambda ref, g: ref[0, g * tq:(g + 1) * tq, :].astype(BF16)

    def weights(zs, laters, bias, tri, masked):
        stacks, betas, masks = [], [], []
        for i in range(n_str):
            z = zs[i] * SCALE + bias[:, :zs[i].shape[1]]
            soft = _softplus(z)
            betas.append(z - soft)
            if masked:
                mask = (lax.broadcasted_iota(jnp.int32, z.shape, 1)
                        < lax.broadcasted_iota(jnp.int32, z.shape, 0) + i * rows)
                soft = jnp.where(mask, soft, 0.0)
                masks.append(mask)
            blocks = [soft[:, j * HEAD_DIM:(j + 1) * HEAD_DIM] for j in range(z.shape[1] // HEAD_DIM)]
            stacks.append(jnp.concatenate(blocks, axis=0))
        sums = [_suffix_and_total(stacks[i], tri) for i in range(n_str)]
        ws, new_laters = [], []
        for i in range(n_str):
            suffix, total = sums[i]
            later = laters[i]
            parts = [None] * (zs[i].shape[1] // HEAD_DIM)
            for j in reversed(range(len(parts))):
                parts[j] = suffix[j * rows:(j + 1) * rows] + later
                later = later + total[j * rows:(j + 1) * rows]
            a = jnp.exp(betas[i] + jnp.concatenate(parts, axis=1))
            if masked:
                a = jnp.where(masks[i], a, 0.0)
            ws.append(a.astype(BF16))
            new_laters.append(later)
        return ws, new_laters

    tri = _suffix_matrix()
    bias = jnp.concatenate([bias_ref[0]] * n_blk, axis=1)

    def query_block(diag):
        q = q_ref[0, diag * tq:(diag + 1) * tq, :].astype(BF16)
        n_keys = lambda g, i: (i + 1) * rows if g == diag else tq
        logits = lambda g: [_dot_nt(q[stream(i)], keys(k_ref, g)[:n_keys(g, i)]) for i in range(n_str)]
        accs = [jnp.zeros((rows, HEAD_DIM), F32) for _ in range(n_str)]
        laters = list(accs)
        zs, ws = logits(diag), None
        for g in range(diag, -1, -1):
            if ws is not None:
                vg = keys(v_ref, g + 1)
                accs = [accs[i] + _dot(ws[i], vg[:n_keys(g + 1, i)]) for i in range(n_str)]
            zs_next = logits(g - 1) if g > 0 else None
            ws, laters = weights(zs, laters, bias, tri, g == diag)
            zs = zs_next
        v0 = keys(v_ref, 0)
        acc = jnp.concatenate([accs[i] + _dot(ws[i], v0[:n_keys(0, i)]) for i in range(n_str)], axis=0)
        o_ref[0, diag * tq:(diag + 1) * tq, :] = (_rms(acc, 1.0) * gain_ref[0]).astype(BF16)

    for diag in range(k_ref.shape[1] // tq):
        query_block(diag)


def sb_prompt(proj3, bias, gain, layer, depth, kv_all):
    b, seq, _ = proj3.shape
    assert seq % SB_QUERY_BLOCK == 0
    col = lambda off: pl.BlockSpec((1, seq, HEAD_DIM), lambda i, h: (i, 0, off + h))
    per_head = pl.BlockSpec((1, 1, HEAD_DIM), lambda i, h: (h, 0, 0))
    kv_out = pl.BlockSpec((None, 1, seq * SB_HEADS, HEAD_DIM), lambda i, h: (layer, i, 0, 0))
    kv_shape = jax.ShapeDtypeStruct((depth, b, seq * SB_HEADS, HEAD_DIM), F32)
    in_specs = [col(COL_SQ), col(COL_SK), col(COL_SV), per_head, per_head]
    args = [proj3, proj3, proj3,
            jnp.broadcast_to(bias[:, None, None], (SB_HEADS, 1, HEAD_DIM)),
            gain.reshape(SB_HEADS, 1, HEAD_DIM)]
    aliases = {}
    if kv_all is not None:
        aliases = {len(args): 1, len(args) + 1: 2}
        in_specs += [pl.BlockSpec(memory_space=pl.ANY)] * 2
        args += list(kv_all)
    sb_out, k_all, v_all = pl.pallas_call(
        _sb_prompt_kernel,
        grid=(b, SB_HEADS),
        in_specs=in_specs,
        out_specs=[pl.BlockSpec((1, seq, HEAD_DIM), lambda i, h: (i, 0, h)), kv_out, kv_out],
        out_shape=[jax.ShapeDtypeStruct((b, seq, SB_DIM), BF16), kv_shape, kv_shape],
        input_output_aliases=aliases,
        compiler_params=_params("parallel", "arbitrary"),
        name="sb_prompt",
    )(*args)
    return sb_out, (k_all, v_all)


DECODE_PAGES_PER_STEP = 16


def _sb_decode_kernel(n_groups, pt_ref, q_ref, *refs):
    del pt_ref
    n_pg = DECODE_PAGES_PER_STEP
    k_refs, v_refs = refs[:n_pg], refs[n_pg:2 * n_pg]
    bias_ref, gain_ref, o_ref, acc_ref, later_ref, a_ref = refs[2 * n_pg:]
    s = pl.program_id(0)
    last = pl.num_programs(0) - 1
    group_k = jnp.minimum(s, last - 1) % n_groups
    group_v = jnp.maximum(s - 1, 0) % n_groups

    @pl.when(s == 0)
    def _():
        acc_ref[...] = jnp.zeros_like(acc_ref)
        later_ref[...] = jnp.zeros_like(later_ref)
        a_ref[...] = jnp.zeros_like(a_ref)

    head = lax.broadcasted_iota(jnp.int32, (SB_HEADS, HEAD_DIM), 0)
    only = lambda x, h: jnp.where(head == h, x, 0.0).astype(BF16)
    head_rows = lambda ref, h: ref[0, 0, pl.ds(h, PAGE_SIZE, stride=SB_HEADS), :].astype(BF16)

    a_prev = a_ref[...]
    acc = jnp.where(group_v == 0, 0.0, acc_ref[...])
    for i, v_ref in enumerate(v_refs):
        a_page = a_prev[i * SB_HEADS:(i + 1) * SB_HEADS]
        out = _dot(only(a_page, 0), head_rows(v_ref, 0))
        for h in range(1, SB_HEADS):
            out = out + _dot(only(a_page, h), head_rows(v_ref, h))
        acc = acc + out
    acc_ref[...] = acc

    q = q_ref[0]
    q_rows = [only(q, h) for h in range(SB_HEADS)]
    zs = []
    for k_ref in k_refs:
        z = _dot_nt(q_rows[0], head_rows(k_ref, 0))
        for h in range(1, SB_HEADS):
            z = z + _dot_nt(q_rows[h], head_rows(k_ref, h))
        zs.append(z)
    bias = jnp.concatenate([bias_ref[...]] * n_pg, axis=0)
    z2 = jnp.concatenate(zs, axis=0) * SCALE + bias
    soft = _softplus(z2)
    suffix, total = _suffix_and_total(soft, _suffix_matrix())
    later = jnp.where(group_k == 0, 0.0, later_ref[...])
    parts = []
    for i in range(n_pg):
        parts.append(suffix[i * SB_HEADS:(i + 1) * SB_HEADS] + later)
        later = later + total[i * SB_HEADS:(i + 1) * SB_HEADS]
    later_ref[...] = later
    a_ref[...] = jnp.exp(z2 - soft + jnp.concatenate(parts, axis=0))

    @pl.when((s > 0) & (group_v == n_groups - 1))
    def _():
        o_ref[0] = _rms(acc_ref[...], 1.0) * gain_ref[...]


def sb_decode(q, cache_k, cache_v, layer, page_table, bias, gain):
    n, n_pages = page_table.shape
    n_pg = DECODE_PAGES_PER_STEP
    n_groups, rem = divmod(n_pages, n_pg)
    assert rem == 0
    n_steps = n * n_groups
    rows = PAGE_SIZE * SB_HEADS
    depth, n_pool = cache_k.shape[:2]
    ck = cache_k.reshape(depth, n_pool, rows, HEAD_DIM)
    cv = cache_v.reshape(depth, n_pool, rows, HEAD_DIM)
    step_k = lambda s: jnp.minimum(s, n_steps - 1)
    step_v = lambda s: jnp.maximum(s - 1, 0)

    def page(step, i_pg):
        def index(s, pt):
            t = step(s)
            return layer, pt[t // n_groups, n_pages - 1 - (n_pg * (t % n_groups) + i_pg)], 0, 0
        return pl.BlockSpec((1, 1, rows, HEAD_DIM), index)

    per_head = pl.BlockSpec((SB_HEADS, HEAD_DIM), lambda s, pt: (0, 0))
    seq_block = lambda step: pl.BlockSpec((1, SB_HEADS, HEAD_DIM), lambda s, pt: (step(s) // n_groups, 0, 0))
    state = pltpu.VMEM((SB_HEADS, HEAD_DIM), F32)
    return pl.pallas_call(
        functools.partial(_sb_decode_kernel, n_groups),
        grid_spec=pltpu.PrefetchScalarGridSpec(
            num_scalar_prefetch=1,
            grid=(n_steps + 1,),
            in_specs=([seq_block(step_k)] + [page(step_k, i) for i in range(n_pg)]
                      + [page(step_v, i) for i in range(n_pg)] + [per_head, per_head]),
            out_specs=seq_block(step_v),
            scratch_shapes=[state, state, pltpu.VMEM((n_pg * SB_HEADS, HEAD_DIM), F32)]),
        out_shape=jax.ShapeDtypeStruct((n, SB_HEADS, HEAD_DIM), F32),
        compiler_params=_params("arbitrary"),
        name="sb_decode",
    )(page_table, q, *([ck] * n_pg), *([cv] * n_pg),
      jnp.broadcast_to(bias[:, None], (SB_HEADS, HEAD_DIM)),
      gain.reshape(SB_HEADS, HEAD_DIM))


def _rotate(x, cos2, sin2):
    return x * cos2 + pltpu.roll(x, HEAD_DIM // 2, 1) * sin2


def _retention_kernel(q_ref, k_ref, v_ref, g_ref, cos_ref, sin_ref, lg_ref, s_ref, o_ref, s_out_ref):
    chunk = float(HEAD_DIM)
    n_chunks = q_ref.shape[1] // HEAD_DIM
    lg = lg_ref[0]
    r = lax.broadcasted_iota(jnp.int32, (HEAD_DIM, HEAD_DIM), 0).astype(F32)
    c = lax.broadcasted_iota(jnp.int32, (HEAD_DIM, HEAD_DIM), 1).astype(F32)
    diff = r - c
    dmat = jnp.where(diff >= 0, jnp.exp(jnp.maximum(diff, 0.0) * lg), 0.0)
    cross = jnp.exp((r + 1.0) * lg)
    k_decay = jnp.exp((chunk - 1.0 - r) * lg)
    s_decay = jnp.exp(chunk * lg)

    def body(i, s):
        rows = slice(i * HEAD_DIM, (i + 1) * HEAD_DIM)
        cos2 = cos_ref[rows, :]
        sin2 = sin_ref[rows, :]
        q = _rotate(q_ref[0, rows, :], cos2, sin2).astype(BF16)
        k = _rotate(k_ref[0, rows, :], cos2, sin2) * SCALE
        v = v_ref[0, rows, :].astype(BF16)
        inner = _dot_nt(q, k.astype(BF16)) * dmat
        o = _dot(inner.astype(BF16), v) + _dot(q, s.astype(BF16)) * cross
        g = g_ref[0, rows, :]
        o_ref[0, rows, :] = (g * jax.nn.sigmoid(g) * _rms(o, 1.0)).astype(o_ref.dtype)
        return s_decay * s + _dot((k * k_decay).T.astype(BF16), v)

    s = s_ref[0, 0]
    for i in range(n_chunks):
        s = body(i, s)
    s_out_ref[0, 0] = s


def retention(proj3, cos2, sin2, log_gamma, state):
    b, rows, _ = proj3.shape
    col = lambda off: pl.BlockSpec((1, rows, HEAD_DIM), lambda i, h: (i, 0, off + h))
    table = pl.BlockSpec((rows, HEAD_DIM), lambda i, h: (0, 0))
    st = pl.BlockSpec((1, 1, HEAD_DIM, HEAD_DIM), lambda i, h: (i, h, 0, 0))
    return pl.pallas_call(
        _retention_kernel,
        grid=(b, RET_HEADS),
        in_specs=[col(COL_RQ), col(COL_RK), col(COL_RV), col(COL_RG), table, table,
                  pl.BlockSpec((1, 1, HEAD_DIM), lambda i, h: (h, 0, 0)), st],
        out_specs=[pl.BlockSpec((1, rows, HEAD_DIM), lambda i, h: (i, 0, h)), st],
        out_shape=[jax.ShapeDtypeStruct((b, rows, RET_DIM), BF16),
                   jax.ShapeDtypeStruct((b, RET_HEADS, HEAD_DIM, HEAD_DIM), F32)],
        compiler_params=_params("parallel", "parallel"),
        name="retention",
    )(proj3, proj3, proj3, proj3, cos2, sin2,
      jnp.broadcast_to(log_gamma[:, None, None], (RET_HEADS, 1, HEAD_DIM)), state)


RET_STEP_ROWS = 8


def _retention_step_kernel(q_ref, k_ref, v_ref, g_ref, cos_ref, sin_ref, lg_ref, s_ref, o_ref, s_out_ref):
    n_rows = q_ref.shape[0]
    cos2, sin2 = cos_ref[...], sin_ref[...]
    fill = jnp.zeros((HEAD_DIM - n_rows, HEAD_DIM), F32)
    columns = lambda x: jnp.concatenate([x, fill], axis=0).T
    for h in range(RET_HEADS):
        lanes = slice(h * HEAD_DIM, (h + 1) * HEAD_DIM)
        gamma = jnp.exp(lg_ref[h])
        q = _rotate(q_ref[:, lanes], cos2, sin2)
        k = _rotate(k_ref[:, lanes], cos2, sin2) * SCALE
        v = v_ref[:, lanes]
        inner = jnp.sum(q * k, axis=-1, keepdims=True)
        q_cols, k_cols = columns(q), columns(k)
        outs = []
        for r in range(n_rows):
            state = s_ref[r, h]
            v_row = v[r:r + 1]
            q_state = jnp.sum(q_cols[:, r:r + 1] * state, axis=0, keepdims=True)
            outs.append(inner[r:r + 1] * v_row + q_state * gamma)
            s_out_ref[r, h] = gamma * state + k_cols[:, r:r + 1] * v_row
        g = g_ref[:, lanes]
        o_ref[:, lanes] = g * jax.nn.sigmoid(g) * _rms(jnp.concatenate(outs, axis=0), 1.0)


def retention_step(proj, cos2, sin2, log_gamma, state):
    n = proj.shape[0]
    rows = RET_STEP_ROWS
    col = lambda j: pl.BlockSpec((rows, RET_DIM), lambda i: (i, COL_RQ * HEAD_DIM // RET_DIM + j))
    table = pl.BlockSpec((1, HEAD_DIM), lambda i: (0, 0))
    st = pl.BlockSpec((rows, RET_HEADS, HEAD_DIM, HEAD_DIM), lambda i: (i, 0, 0, 0))
    return pl.pallas_call(
        _retention_step_kernel,
        grid=(n // rows,),
        in_specs=[col(0), col(1), col(2), col(3), table, table,
                  pl.BlockSpec((RET_HEADS, 1, HEAD_DIM), lambda i: (0, 0, 0)), st],
        out_specs=[pl.BlockSpec((rows, RET_DIM), lambda i: (i, 0)), st],
        out_shape=[jax.ShapeDtypeStruct((n, RET_DIM), F32),
                   jax.ShapeDtypeStruct((n, RET_HEADS, HEAD_DIM, HEAD_DIM), F32)],
        compiler_params=_params("parallel"),
        name="retention_step",
    )(proj, proj, proj, proj, cos2, sin2,
      jnp.broadcast_to(log_gamma[:, None, None], (RET_HEADS, 1, HEAD_DIM)), state)


def _rotary_tables(pos):
    inv = ROPE_BASE ** (-jnp.arange(0, HEAD_DIM, 2, dtype=F32) / HEAD_DIM)
    ang = pos.astype(F32)[:, None] * inv[None, :]
    cos, sin = jnp.cos(ang), jnp.sin(ang)
    return jnp.concatenate([cos, cos], axis=-1), jnp.concatenate([-sin, sin], axis=-1)


def _tiles(m):
    return {"tm": min(m, 512), "tm_wide": min(m, 1024), "tm_down": min(m, 256), "tn_in": IN_COLS // 4, "tf": 512}


def _log_gamma():
    return jnp.log1p(-jnp.exp2(-5.0 - jnp.arange(RET_HEADS, dtype=F32)))


def _mix_prompt(proj, b, seq, p, layer, depth, kv_all):
    proj3 = proj.reshape(b, seq, IN_COLS)
    conv_out, conv_new = conv_prompt(proj3, jnp.zeros((b, CONV_WIDTH - 1, CONV_DIM), F32), p["conv_w"])
    sb_out, kv_all = sb_prompt(proj3, p["sb_bias"], p["sb_gain"], layer, depth, kv_all)
    cos2, sin2 = _rotary_tables(jnp.arange(seq, dtype=jnp.int32))
    ret_out, ret_new = retention(proj3, cos2, sin2, _log_gamma(),
                                 jnp.zeros((b, RET_HEADS, HEAD_DIM, HEAD_DIM), F32))
    mix = (conv_out.reshape(b * seq, CONV_DIM), sb_out.reshape(b * seq, SB_DIM), ret_out.reshape(b * seq, RET_DIM))
    return mix, kv_all, conv_new, ret_new


def _mix_sample(proj, p, cache_k, cache_v, layer, state_conv, state_ret, page_table):
    m = proj.shape[0]
    conv_out, conv_new = conv_step(proj, state_conv, p["conv_w"])
    heads = lambda col: proj[:, col * HEAD_DIM:(col + SB_HEADS) * HEAD_DIM].reshape(m, SB_HEADS, HEAD_DIM)
    sb_out = sb_decode(heads(COL_SQ), cache_k, cache_v, layer, page_table, p["sb_bias"], p["sb_gain"])
    past_len = page_table.shape[1] * PAGE_SIZE
    cos2, sin2 = _rotary_tables(jnp.full((1,), past_len, jnp.int32))
    ret_out, ret_new = retention_step(proj, cos2, sin2, _log_gamma(), state_ret)
    mix = (conv_out, sb_out.reshape(m, SB_DIM).astype(BF16), ret_out.astype(BF16))
    return mix, heads(COL_SK)[:, None], heads(COL_SV)[:, None], conv_new, ret_new


def kernel(x_prompt, x_sample, cache_k, cache_v, state_conv, state_ret, page_table, norm_mix_pre, norm_mix_post, norm_ffn_pre, norm_ffn_post, w_in, conv_w, sb_bias, sb_gain, w_out, w_gate_up, w_down):
    b, seq, _ = x_prompt.shape
    n = x_sample.shape[0]
    depth = w_in.shape[0]
    xp = x_prompt.reshape(b * seq, D_MODEL)
    xs = x_sample.reshape(n, D_MODEL)
    xpn = norm_cast(xp, norm_mix_pre[0], _tiles(b * seq)["tm"])
    xsn = norm_cast(xs, norm_mix_pre[0], _tiles(n)["tm"])
    kv_all = None
    outs_p, outs_s = [], []
    w_out_bf, w_down_bf = w_out.astype(BF16), w_down.astype(BF16)
    t = _tiles(b * seq)
    for l in range(depth):
        p = {"conv_w": conv_w[l], "sb_bias": sb_bias[l], "sb_gain": sb_gain[l],
             "g_mix_post": norm_mix_post[l], "g_ffn_pre": norm_ffn_pre[l], "g_ffn_post": norm_ffn_post[l]}
        g_next = norm_mix_pre[l + 1] if l + 1 < depth else None
        proj_p, proj_s = in_proj(xpn, xsn, w_in, l, t["tm"], t["tn_in"])
        mix_p, kv_all, *rest_p = _mix_prompt(proj_p, b, seq, p, l, depth, kv_all)
        mix_s, *rest_s = _mix_sample(proj_s, p, cache_k, cache_v, l, state_conv[l], state_ret[l], page_table)
        xp, xpn, xs, xsn = out_proj(mix_p, xp, mix_s, xs, w_out_bf, l, p["g_mix_post"], p["g_ffn_pre"], t["tm"])
        hp, hs = gate_up(xpn, xsn, w_gate_up, l, t["tm_wide"], t["tf"])
        xp, xs, xpn, xsn = down_proj(hp, xp, hs, xs, w_down_bf, l, p["g_ffn_post"], g_next, t["tm_down"])
        outs_p.append(rest_p)
        outs_s.append(rest_s)
    stack = lambda outs, i: jnp.stack([o[i] for o in outs])
    k_prompt, v_prompt = (a.reshape(depth, b, seq, SB_HEADS, HEAD_DIM) for a in kv_all)
    return (xp.reshape(b, seq, D_MODEL), xs.reshape(n, 1, D_MODEL),
            k_prompt, v_prompt, stack(outs_p, 0), stack(outs_p, 1),
            stack(outs_s, 0), stack(outs_s, 1), stack(outs_s, 2), stack(outs_s, 3))
```

```python
import functools

import jax
import jax.numpy as jnp
from jax import lax
from jax.experimental import pallas as pl
from jax.experimental.pallas import tpu as pltpu

D_MODEL = 2048
HEAD_DIM = 128
CONV_DIM = 512
SB_DIM = 1024
RET_DIM = 512
SB_HEADS = 8
RET_HEADS = 4
CONV_WIDTH = 3
IN_COLS = 3 * CONV_DIM + 3 * SB_DIM + 4 * RET_DIM
FFN_DIM = 5632
PAGE_SIZE = 128
ROPE_BASE = 10000.0
RMS_EPS = 1e-6
SCALE = HEAD_DIM ** -0.5

COL_CB, COL_CC, COL_CH = 0, 4, 8
COL_SQ, COL_SK, COL_SV = 12, 20, 28
COL_RQ, COL_RK, COL_RV, COL_RG = 36, 40, 44, 48

VMEM_LIMIT_BYTES = 56 * 1024 * 1024
BF16 = jnp.bfloat16
F32 = jnp.float32


def _params(*semantics):
    return pltpu.CompilerParams(dimension_semantics=semantics, vmem_limit_bytes=VMEM_LIMIT_BYTES)


def _rms(x, gain):
    return x * lax.rsqrt(jnp.mean(x * x, axis=-1, keepdims=True) + RMS_EPS) * gain


def _softplus(z):
    return jnp.maximum(z, 0.0) + jnp.log(1.0 + jnp.exp(-jnp.abs(z)))


def _dot(a, b):
    return jnp.dot(a, b, preferred_element_type=F32)


def _dot_nt(a, b):
    return lax.dot_general(a, b, (((1,), (1,)), ((), ())), preferred_element_type=F32)


def _suffix_matrix():
    r = lax.broadcasted_iota(jnp.int32, (2 * HEAD_DIM, 2 * HEAD_DIM), 0) % HEAD_DIM
    c = lax.broadcasted_iota(jnp.int32, (2 * HEAD_DIM, 2 * HEAD_DIM), 1)
    return jnp.where((r > c) | (c >= HEAD_DIM), -1.0, 0.0).astype(BF16)


def _suffix_and_total(soft, tri):
    hi = soft.astype(BF16)
    lo = (soft - hi.astype(F32)).astype(BF16)
    s = _dot(jnp.concatenate([hi, lo], axis=1), tri)
    return s[:, :HEAD_DIM], s[:, HEAD_DIM:]


def _in_proj_kernel(normalize, a_ref, a2_ref, w_ref, *rest):
    if normalize:
        g_ref, o_ref, o2_ref, wb_ref = rest
        prepare = lambda ref: _rms(ref[...], g_ref[...]).astype(BF16)
    else:
        o_ref, o2_ref, wb_ref = rest
        prepare = lambda ref: ref[...]

    @pl.when(pl.program_id(1) == 0)
    def _():
        wb_ref[...] = w_ref[...].astype(BF16)
        o2_ref[...] = _dot(prepare(a2_ref), wb_ref[...])

    o_ref[...] = _dot(prepare(a_ref), wb_ref[...])


def in_proj(a, a2, w, layer, tm, tn, gain=None):
    m, k = a.shape
    m2 = a2.shape[0]
    n = w.shape[2]
    normalize = gain is not None
    in_specs = [pl.BlockSpec((tm, k), lambda j, i: (i, 0)),
                pl.BlockSpec((m2, k), lambda j, i: (0, 0)),
                pl.BlockSpec((None, k, tn), lambda j, i: (layer, 0, j))]
    args = [a, a2, w]
    if normalize:
        in_specs.append(pl.BlockSpec((1, k), lambda j, i: (0, 0)))
        args.append(gain.reshape(1, k))
    return pl.pallas_call(
        functools.partial(_in_proj_kernel, normalize),
        grid=(n // tn, m // tm),
        in_specs=in_specs,
        out_specs=[pl.BlockSpec((tm, tn), lambda j, i: (i, j)),
                   pl.BlockSpec((m2, tn), lambda j, i: (0, j))],
        out_shape=[jax.ShapeDtypeStruct((m, n), F32), jax.ShapeDtypeStruct((m2, n), F32)],
        scratch_shapes=[pltpu.VMEM((k, tn), BF16)],
        compiler_params=_params("arbitrary", "arbitrary"),
        name="in_proj",
    )(*args)


def _out_proj_kernel(c_ref, s_ref, r_ref, x_ref, c2_ref, s2_ref, r2_ref, x2_ref, w_ref, gpost_ref, gnext_ref,
                     x_out_ref, xn_out_ref, x2_out_ref, xn2_out_ref):
    def project(c, s, r):
        return _dot(jnp.concatenate([c, s, r], axis=-1), w_ref[...])

    def finish(y, x):
        x = x + _rms(y, gpost_ref[...])
        return x, _rms(x, gnext_ref[...]).astype(BF16)

    @pl.when(pl.program_id(0) == 0)
    def _():
        x2_out_ref[...], xn2_out_ref[...] = finish(project(c2_ref[...], s2_ref[...], r2_ref[...]), x2_ref[...])

    half = c_ref.shape[0] // 2
    halves = [slice(0, half), slice(half, 2 * half)]
    ys = [project(c_ref[rows, :], s_ref[rows, :], r_ref[rows, :]) for rows in halves]
    for rows, y in zip(halves, ys):
        x_out_ref[rows, :], xn_out_ref[rows, :] = finish(y, x_ref[rows, :])


def out_proj(mix, x, mix2, x2, w, layer, g_post, g_next, tm):
    m, m2 = x.shape[0], x2.shape[0]
    row = lambda i: (i, 0)
    fixed = lambda i: (0, 0)
    widths = (CONV_DIM, SB_DIM, RET_DIM, D_MODEL)
    return pl.pallas_call(
        _out_proj_kernel,
        grid=(m // tm,),
        in_specs=([pl.BlockSpec((tm, n), row) for n in widths] + [pl.BlockSpec((m2, n), fixed) for n in widths]
                  + [pl.BlockSpec((None, D_MODEL, D_MODEL), lambda i: (layer, 0, 0)),
                     pl.BlockSpec((1, D_MODEL), fixed), pl.BlockSpec((1, D_MODEL), fixed)]),
        out_specs=[pl.BlockSpec((tm, D_MODEL), row), pl.BlockSpec((tm, D_MODEL), row),
                   pl.BlockSpec((m2, D_MODEL), fixed), pl.BlockSpec((m2, D_MODEL), fixed)],
        out_shape=[jax.ShapeDtypeStruct((m, D_MODEL), F32), jax.ShapeDtypeStruct((m, D_MODEL), BF16),
                   jax.ShapeDtypeStruct((m2, D_MODEL), F32), jax.ShapeDtypeStruct((m2, D_MODEL), BF16)],
        compiler_params=_params("arbitrary"),
        name="out_proj",
    )(*mix, x, *mix2, x2, w, g_post.reshape(1, -1), g_next.reshape(1, -1))


def _gate_up_kernel(a_ref, a2_ref, wg_ref, wu_ref, o_ref, o2_ref, wgb_ref, wub_ref):
    def swiglu(a):
        gate = _dot(a, wgb_ref[...])
        up = _dot(a, wub_ref[...])
        return (gate * jax.nn.sigmoid(gate) * up).astype(BF16)

    @pl.when(pl.program_id(1) == 0)
    def _():
        wgb_ref[...] = wg_ref[...].astype(BF16)
        wub_ref[...] = wu_ref[...].astype(BF16)
        o2_ref[...] = swiglu(a2_ref[...])

    o_ref[...] = swiglu(a_ref[...])


def gate_up(a, a2, w, layer, tm, tf):
    m, m2 = a.shape[0], a2.shape[0]
    nf = FFN_DIM // tf
    return pl.pallas_call(
        _gate_up_kernel,
        grid=(nf, m // tm),
        in_specs=[pl.BlockSpec((tm, D_MODEL), lambda j, i: (i, 0)),
                  pl.BlockSpec((m2, D_MODEL), lambda j, i: (0, 0)),
                  pl.BlockSpec((None, D_MODEL, tf), lambda j, i: (layer, 0, j)),
                  pl.BlockSpec((None, D_MODEL, tf), lambda j, i: (layer, 0, j + nf))],
        out_specs=[pl.BlockSpec((tm, tf), lambda j, i: (i, j)),
                   pl.BlockSpec((m2, tf), lambda j, i: (0, j))],
        out_shape=[jax.ShapeDtypeStruct((m, FFN_DIM), BF16), jax.ShapeDtypeStruct((m2, FFN_DIM), BF16)],
        scratch_shapes=[pltpu.VMEM((D_MODEL, tf), BF16), pltpu.VMEM((D_MODEL, tf), BF16)],
        compiler_params=_params("arbitrary", "arbitrary"),
        name="gate_up",
    )(a, a2, w, w)


def _down_kernel(emit_next, h_ref, x_ref, h2_ref, x2_ref, w_ref, gpost_ref, *rest):
    if emit_next:
        gnext_ref, x_out_ref, x2_out_ref, xn_out_ref, xn2_out_ref = rest
    else:
        x_out_ref, x2_out_ref = rest

    def finish(h, x, out_ref, norm_ref):
        x = x + _rms(_dot(h, w_ref[...]), gpost_ref[...])
        out_ref[...] = x
        if emit_next:
            norm_ref[...] = _rms(x, gnext_ref[...]).astype(BF16)

    @pl.when(pl.program_id(0) == 0)
    def _():
        finish(h2_ref[...], x2_ref[...], x2_out_ref, xn2_out_ref if emit_next else None)

    finish(h_ref[...], x_ref[...], x_out_ref, xn_out_ref if emit_next else None)


def down_proj(h, x, h2, x2, w, layer, g_post, g_next, tm):
    m, m2 = x.shape[0], x2.shape[0]
    emit_next = g_next is not None
    row = lambda i: (i, 0)
    fixed = lambda i: (0, 0)
    in_specs = [pl.BlockSpec((tm, FFN_DIM), row), pl.BlockSpec((tm, D_MODEL), row),
                pl.BlockSpec((m2, FFN_DIM), fixed), pl.BlockSpec((m2, D_MODEL), fixed),
                pl.BlockSpec((None, FFN_DIM, D_MODEL), lambda i: (layer, 0, 0), pipeline_mode=pl.Buffered(1)),
                pl.BlockSpec((1, D_MODEL), fixed)]
    args = [h, x, h2, x2, w, g_post.reshape(1, -1)]
    out_specs = [pl.BlockSpec((tm, D_MODEL), row), pl.BlockSpec((m2, D_MODEL), fixed)]
    out_shape = [jax.ShapeDtypeStruct((m, D_MODEL), F32), jax.ShapeDtypeStruct((m2, D_MODEL), F32)]
    if emit_next:
        in_specs.append(pl.BlockSpec((1, D_MODEL), fixed))
        args.append(g_next.reshape(1, -1))
        out_specs += [pl.BlockSpec((tm, D_MODEL), row), pl.BlockSpec((m2, D_MODEL), fixed)]
        out_shape += [jax.ShapeDtypeStruct((m, D_MODEL), BF16), jax.ShapeDtypeStruct((m2, D_MODEL), BF16)]
    outs = pl.pallas_call(
        functools.partial(_down_kernel, emit_next),
        grid=(m // tm,),
        in_specs=in_specs,
        out_specs=out_specs,
        out_shape=out_shape,
        compiler_params=_params("arbitrary"),
        name="down_proj",
    )(*args)
    return tuple(outs) if emit_next else (outs[0], outs[1], None, None)


def _conv_prompt_kernel(cb_ref, cc_ref, ch_ref, st_ref, w_ref, o_ref, new_ref):
    u = cc_ref[0] * ch_ref[0]
    seq = u.shape[0]
    st = st_ref[0]
    w = w_ref[...]
    row = lax.broadcasted_iota(jnp.int32, u.shape, 0)
    back1 = jnp.where(row == 0, st[1:2], pltpu.roll(u, 1, 0))
    back2 = jnp.where(row == 0, st[0:1], jnp.where(row == 1, st[1:2], pltpu.roll(u, 2, 0)))
    y = back2 * w[0:1] + back1 * w[1:2] + u * w[2:3]
    o_ref[0] = (cb_ref[0] * y).astype(BF16)
    new_ref[0] = u[seq - 2:, :]


def conv_prompt(proj3, state, conv_w):
    b, seq, _ = proj3.shape
    col = lambda off: pl.BlockSpec((1, seq, HEAD_DIM), lambda i, c: (i, 0, off + c))
    return pl.pallas_call(
        _conv_prompt_kernel,
        grid=(b, CONV_DIM // HEAD_DIM),
        in_specs=[col(COL_CB), col(COL_CC), col(COL_CH),
                  pl.BlockSpec((1, CONV_WIDTH - 1, HEAD_DIM), lambda i, c: (i, 0, c)),
                  pl.BlockSpec((CONV_WIDTH, HEAD_DIM), lambda i, c: (0, c))],
        out_specs=[pl.BlockSpec((1, seq, HEAD_DIM), lambda i, c: (i, 0, c)),
                   pl.BlockSpec((1, CONV_WIDTH - 1, HEAD_DIM), lambda i, c: (i, 0, c))],
        out_shape=[jax.ShapeDtypeStruct((b, seq, CONV_DIM), BF16),
                   jax.ShapeDtypeStruct((b, CONV_WIDTH - 1, CONV_DIM), F32)],
        compiler_params=_params("parallel", "parallel"),
        name="conv_prompt",
    )(proj3, proj3, proj3, state, conv_w)


def _conv_step_kernel(cb_ref, cc_ref, ch_ref, s0_ref, s1_ref, w_ref, o_ref, u_ref):
    u = cc_ref[...] * ch_ref[...]
    w = w_ref[...]
    y = s0_ref[...] * w[0:1] + s1_ref[...] * w[1:2] + u * w[2:3]
    o_ref[...] = (cb_ref[...] * y).astype(BF16)
    u_ref[...] = u


def conv_step(proj, state, conv_w):
    n = proj.shape[0]
    col = lambda j: pl.BlockSpec((n, CONV_DIM), lambda i: (0, j))
    full = pl.BlockSpec((n, CONV_DIM), lambda i: (0, 0))
    out, u = pl.pallas_call(
        _conv_step_kernel,
        grid=(1,),
        in_specs=[col(0), col(1), col(2), full, full,
                  pl.BlockSpec((CONV_WIDTH, CONV_DIM), lambda i: (0, 0))],
        out_specs=[full, full],
        out_shape=[jax.ShapeDtypeStruct((n, CONV_DIM), BF16), jax.ShapeDtypeStruct((n, CONV_DIM), F32)],
        compiler_params=_params("arbitrary"),
        name="conv_step",
    )(proj, proj, proj, state[:, 0], state[:, 1], conv_w)
    return out, jnp.stack([state[:, 1], u], axis=1)


SB_QUERY_BLOCK = 4 * HEAD_DIM


def _sb_prompt_kernel(q_ref, k_ref, v_ref, bias_ref, gain_ref, *rest):
    o_ref, k_out_ref, v_out_ref = rest[-3:]
    tq = SB_QUERY_BLOCK
    head_rows = pl.ds(pl.program_id(1), k_ref.shape[1], stride=SB_HEADS)
    k_out_ref[0, head_rows, :] = k_ref[0]
    v_out_ref[0, head_rows, :] = v_ref[0]

    n_blk = tq // HEAD_DIM
    n_str = 2
    rows = tq // n_str
    stream = lambda i: slice(i * rows, (i + 1) * rows)
    keys = lambda ref, g: ref[0, g * tq:(g + 1) * tq, :].astype(BF16)

    def weights(zs, laters, bias, tri, masked):
        stacks, betas, masks = [], [], []
        for i in range(n_str):
            z = zs[i] * SCALE + bias[:, :zs[i].shape[1]]
            soft = _softplus(z)
            betas.append(z - soft)
            if masked:
                mask = (lax.broadcasted_iota(jnp.int32, z.shape, 1)
                        < lax.broadcasted_iota(jnp.int32, z.shape, 0) + i * rows)
                soft = jnp.where(mask, soft, 0.0)
                masks.append(mask)
            blocks = [soft[:, j * HEAD_DIM:(j + 1) * HEAD_DIM] for j in range(z.shape[1] // HEAD_DIM)]
            stacks.append(jnp.concatenate(blocks, axis=0))
        sums = [_suffix_and_total(stacks[i], tri) for i in range(n_str)]
        ws, new_laters = [], []
        for i in range(n_str):
            suffix, total = sums[i]
            later = laters[i]
            parts = [None] * (zs[i].shape[1] // HEAD_DIM)
            for j in reversed(range(len(parts))):
                parts[j] = suffix[j * rows:(j + 1) * rows] + later
                later = later + total[j * rows:(j + 1) * rows]
            a = jnp.exp(betas[i] + jnp.concatenate(parts, axis=1))
            if masked:
                a = jnp.where(masks[i], a, 0.0)
            ws.append(a.astype(BF16))
            new_laters.append(later)
        return ws, new_laters

    tri = _suffix_matrix()
    bias = jnp.concatenate([bias_ref[0]] * n_blk, axis=1)

    def query_block(diag):
        q = q_ref[0, diag * tq:(diag + 1) * tq, :].astype(BF16)
        n_keys = lambda g, i: (i + 1) * rows if g == diag else tq
        logits = lambda g: [_dot_nt(q[stream(i)], keys(k_ref, g)[:n_keys(g, i)]) for i in range(n_str)]
        accs = [jnp.zeros((rows, HEAD_DIM), F32) for _ in range(n_str)]
        laters = list(accs)
        zs, ws = logits(diag), None
        for g in range(diag, -1, -1):
            if ws is not None:
                vg = keys(v_ref, g + 1)
                accs = [accs[i] + _dot(ws[i], vg[:n_keys(g + 1, i)]) for i in range(n_str)]
            zs_next = logits(g - 1) if g > 0 else None
            ws, laters = weights(zs, laters, bias, tri, g == diag)
            zs = zs_next
        v0 = keys(v_ref, 0)
        acc = jnp.concatenate([accs[i] + _dot(ws[i], v0[:n_keys(0, i)]) for i in range(n_str)], axis=0)
        o_ref[0, diag * tq:(diag + 1) * tq, :] = (_rms(acc, 1.0) * gain_ref[0]).astype(BF16)

    for diag in range(k_ref.shape[1] // tq):
        query_block(diag)


def sb_prompt(proj3, bias, gain, layer, depth, kv_all):
    b, seq, _ = proj3.shape
    assert seq % SB_QUERY_BLOCK == 0
    col = lambda off: pl.BlockSpec((1, seq, HEAD_DIM), lambda i, h: (i, 0, off + h))
    per_head = pl.BlockSpec((1, 1, HEAD_DIM), lambda i, h: (h, 0, 0))
    kv_out = pl.BlockSpec((None, 1, seq * SB_HEADS, HEAD_DIM), lambda i, h: (layer, i, 0, 0))
    kv_shape = jax.ShapeDtypeStruct((depth, b, seq * SB_HEADS, HEAD_DIM), F32)
    in_specs = [col(COL_SQ), col(COL_SK), col(COL_SV), per_head, per_head]
    args = [proj3, proj3, proj3,
            jnp.broadcast_to(bias[:, None, None], (SB_HEADS, 1, HEAD_DIM)),
            gain.reshape(SB_HEADS, 1, HEAD_DIM)]
    aliases = {}
    if kv_all is not None:
        aliases = {len(args): 1, len(args) + 1: 2}
        in_specs += [pl.BlockSpec(memory_space=pl.ANY)] * 2
        args += list(kv_all)
    sb_out, k_all, v_all = pl.pallas_call(
        _sb_prompt_kernel,
        grid=(b, SB_HEADS),
        in_specs=in_specs,
        out_specs=[pl.BlockSpec((1, seq, HEAD_DIM), lambda i, h: (i, 0, h)), kv_out, kv_out],
        out_shape=[jax.ShapeDtypeStruct((b, seq, SB_DIM), BF16), kv_shape, kv_shape],
        input_output_aliases=aliases,
        compiler_params=_params("parallel", "arbitrary"),
        name="sb_prompt",
    )(*args)
    return sb_out, (k_all, v_all)


DECODE_PAGES_PER_STEP = 16


def _sb_decode_kernel(n_groups, pt_ref, q_ref, *refs):
    del pt_ref
    n_pg = DECODE_PAGES_PER_STEP
    k_refs, v_refs = refs[:n_pg], refs[n_pg:2 * n_pg]
    bias_ref, gain_ref, o_ref, acc_ref, later_ref, a_ref = refs[2 * n_pg:]
    s = pl.program_id(0)
    last = pl.num_programs(0) - 1
    group_k = jnp.minimum(s, last - 1) % n_groups
    group_v = jnp.maximum(s - 1, 0) % n_groups

    @pl.when(s == 0)
    def _():
        acc_ref[...] = jnp.zeros_like(acc_ref)
        later_ref[...] = jnp.zeros_like(later_ref)
        a_ref[...] = jnp.zeros_like(a_ref)

    head = lax.broadcasted_iota(jnp.int32, (SB_HEADS, HEAD_DIM), 0)
    only = lambda x, h: jnp.where(head == h, x, 0.0).astype(BF16)
    head_rows = lambda ref, h: ref[0, 0, pl.ds(h, PAGE_SIZE, stride=SB_HEADS), :].astype(BF16)

    a_prev = a_ref[...]
    acc = jnp.where(group_v == 0, 0.0, acc_ref[...])
    for i, v_ref in enumerate(v_refs):
        a_page = a_prev[i * SB_HEADS:(i + 1) * SB_HEADS]
        out = _dot(only(a_page, 0), head_rows(v_ref, 0))
        for h in range(1, SB_HEADS):
            out = out + _dot(only(a_page, h), head_rows(v_ref, h))
        acc = acc + out
    acc_ref[...] = acc

    q = q_ref[0]
    q_rows = [only(q, h) for h in range(SB_HEADS)]
    zs = []
    for k_ref in k_refs:
        z = _dot_nt(q_rows[0], head_rows(k_ref, 0))
        for h in range(1, SB_HEADS):
            z = z + _dot_nt(q_rows[h], head_rows(k_ref, h))
        zs.append(z)
    bias = jnp.concatenate([bias_ref[...]] * n_pg, axis=0)
    z2 = jnp.concatenate(zs, axis=0) * SCALE + bias
    soft = _softplus(z2)
    suffix, total = _suffix_and_total(soft, _suffix_matrix())
    later = jnp.where(group_k == 0, 0.0, later_ref[...])
    parts = []
    for i in range(n_pg):
        parts.append(suffix[i * SB_HEADS:(i + 1) * SB_HEADS] + later)
        later = later + total[i * SB_HEADS:(i + 1) * SB_HEADS]
    later_ref[...] = later
    a_ref[...] = jnp.exp(z2 - soft + jnp.concatenate(parts, axis=0))

    @pl.when((s > 0) & (group_v == n_groups - 1))
    def _():
        o_ref[0] = _rms(acc_ref[...], 1.0) * gain_ref[...]


def sb_decode(q, cache_k, cache_v, layer, page_table, bias, gain):
    n, n_pages = page_table.shape
    n_pg = DECODE_PAGES_PER_STEP
    n_groups, rem = divmod(n_pages, n_pg)
    assert rem == 0
    n_steps = n * n_groups
    rows = PAGE_SIZE * SB_HEADS
    depth, n_pool = cache_k.shape[:2]
    ck = cache_k.reshape(depth, n_pool, rows, HEAD_DIM)
    cv = cache_v.reshape(depth, n_pool, rows, HEAD_DIM)
    step_k = lambda s: jnp.minimum(s, n_steps - 1)
    step_v = lambda s: jnp.maximum(s - 1, 0)

    def page(step, i_pg):
        def index(s, pt):
            t = step(s)
            return layer, pt[t // n_groups, n_pages - 1 - (n_pg * (t % n_groups) + i_pg)], 0, 0
        return pl.BlockSpec((1, 1, rows, HEAD_DIM), index)

    per_head = pl.BlockSpec((SB_HEADS, HEAD_DIM), lambda s, pt: (0, 0))
    seq_block = lambda step: pl.BlockSpec((1, SB_HEADS, HEAD_DIM), lambda s, pt: (step(s) // n_groups, 0, 0))
    state = pltpu.VMEM((SB_HEADS, HEAD_DIM), F32)
    return pl.pallas_call(
        functools.partial(_sb_decode_kernel, n_groups),
        grid_spec=pltpu.PrefetchScalarGridSpec(
            num_scalar_prefetch=1,
            grid=(n_steps + 1,),
            in_specs=([seq_block(step_k)] + [page(step_k, i) for i in range(n_pg)]
                      + [page(step_v, i) for i in range(n_pg)] + [per_head, per_head]),
            out_specs=seq_block(step_v),
            scratch_shapes=[state, state, pltpu.VMEM((n_pg * SB_HEADS, HEAD_DIM), F32)]),
        out_shape=jax.ShapeDtypeStruct((n, SB_HEADS, HEAD_DIM), F32),
        compiler_params=_params("arbitrary"),
        name="sb_decode",
    )(page_table, q, *([ck] * n_pg), *([cv] * n_pg),
      jnp.broadcast_to(bias[:, None], (SB_HEADS, HEAD_DIM)),
      gain.reshape(SB_HEADS, HEAD_DIM))


def _rotate(x, cos2, sin2):
    return x * cos2 + pltpu.roll(x, HEAD_DIM // 2, 1) * sin2


def _retention_kernel(q_ref, k_ref, v_ref, g_ref, cos_ref, sin_ref, lg_ref, s_ref, o_ref, s_out_ref):
    chunk = float(HEAD_DIM)
    n_chunks = q_ref.shape[1] // HEAD_DIM
    lg = lg_ref[0]
    r = lax.broadcasted_iota(jnp.int32, (HEAD_DIM, HEAD_DIM), 0).astype(F32)
    c = lax.broadcasted_iota(jnp.int32, (HEAD_DIM, HEAD_DIM), 1).astype(F32)
    diff = r - c
    dmat = jnp.where(diff >= 0, jnp.exp(jnp.maximum(diff, 0.0) * lg), 0.0)
    cross = jnp.exp((r + 1.0) * lg)
    k_decay = jnp.exp((chunk - 1.0 - r) * lg)
    s_decay = jnp.exp(chunk * lg)

    def body(i, s):
        rows = slice(i * HEAD_DIM, (i + 1) * HEAD_DIM)
        cos2 = cos_ref[rows, :]
        sin2 = sin_ref[rows, :]
        q = _rotate(q_ref[0, rows, :], cos2, sin2).astype(BF16)
        k = _rotate(k_ref[0, rows, :], cos2, sin2) * SCALE
        v = v_ref[0, rows, :].astype(BF16)
        inner = _dot_nt(q, k.astype(BF16)) * dmat
        o = _dot(inner.astype(BF16), v) + _dot(q, s.astype(BF16)) * cross
        g = g_ref[0, rows, :]
        o_ref[0, rows, :] = (g * jax.nn.sigmoid(g) * _rms(o, 1.0)).astype(o_ref.dtype)
        return s_decay * s + _dot((k * k_decay).T.astype(BF16), v)

    s = s_ref[0, 0]
    for i in range(n_chunks):
        s = body(i, s)
    s_out_ref[0, 0] = s


def retention(proj3, cos2, sin2, log_gamma, state):
    b, rows, _ = proj3.shape
    col = lambda off: pl.BlockSpec((1, rows, HEAD_DIM), lambda i, h: (i, 0, off + h))
    table = pl.BlockSpec((rows, HEAD_DIM), lambda i, h: (0, 0))
    st = pl.BlockSpec((1, 1, HEAD_DIM, HEAD_DIM), lambda i, h: (i, h, 0, 0))
    return pl.pallas_call(
        _retention_kernel,
        grid=(b, RET_HEADS),
        in_specs=[col(COL_RQ), col(COL_RK), col(COL_RV), col(COL_RG), table, table,
                  pl.BlockSpec((1, 1, HEAD_DIM), lambda i, h: (h, 0, 0)), st],
        out_specs=[pl.BlockSpec((1, rows, HEAD_DIM), lambda i, h: (i, 0, h)), st],
        out_shape=[jax.ShapeDtypeStruct((b, rows, RET_DIM), BF16),
                   jax.ShapeDtypeStruct((b, RET_HEADS, HEAD_DIM, HEAD_DIM), F32)],
        compiler_params=_params("parallel", "parallel"),
        name="retention",
    )(proj3, proj3, proj3, proj3, cos2, sin2,
      jnp.broadcast_to(log_gamma[:, None, None], (RET_HEADS, 1, HEAD_DIM)), state)


RET_STEP_ROWS = 8


def _retention_step_kernel(q_ref, k_ref, v_ref, g_ref, cos_ref, sin_ref, lg_ref, s_ref, o_ref, s_out_ref):
    n_rows = q_ref.shape[0]
    cos2, sin2 = cos_ref[...], sin_ref[...]
    fill = jnp.zeros((HEAD_DIM - n_rows, HEAD_DIM), F32)
    columns = lambda x: jnp.concatenate([x, fill], axis=0).T
    for h in range(RET_HEADS):
        lanes = slice(h * HEAD_DIM, (h + 1) * HEAD_DIM)
        gamma = jnp.exp(lg_ref[h])
        q = _rotate(q_ref[:, lanes], cos2, sin2)
        k = _rotate(k_ref[:, lanes], cos2, sin2) * SCALE
        v = v_ref[:, lanes]
        inner = jnp.sum(q * k, axis=-1, keepdims=True)
        q_cols, k_cols = columns(q), columns(k)
        outs = []
        for r in range(n_rows):
            state = s_ref[r, h]
            v_row = v[r:r + 1]
            q_state = jnp.sum(q_cols[:, r:r + 1] * state, axis=0, keepdims=True)
            outs.append(inner[r:r + 1] * v_row + q_state * gamma)
            s_out_ref[r, h] = gamma * state + k_cols[:, r:r + 1] * v_row
        g = g_ref[:, lanes]
        o_ref[:, lanes] = g * jax.nn.sigmoid(g) * _rms(jnp.concatenate(outs, axis=0), 1.0)


def retention_step(proj, cos2, sin2, log_gamma, state):
    n = proj.shape[0]
    rows = RET_STEP_ROWS
    col = lambda j: pl.BlockSpec((rows, RET_DIM), lambda i: (i, COL_RQ * HEAD_DIM // RET_DIM + j))
    table = pl.BlockSpec((1, HEAD_DIM), lambda i: (0, 0))
    st = pl.BlockSpec((rows, RET_HEADS, HEAD_DIM, HEAD_DIM), lambda i: (i, 0, 0, 0))
    return pl.pallas_call(
        _retention_step_kernel,
        grid=(n // rows,),
        in_specs=[col(0), col(1), col(2), col(3), table, table,
                  pl.BlockSpec((RET_HEADS, 1, HEAD_DIM), lambda i: (0, 0, 0)), st],
        out_specs=[pl.BlockSpec((rows, RET_DIM), lambda i: (i, 0)), st],
        out_shape=[jax.ShapeDtypeStruct((n, RET_DIM), F32),
                   jax.ShapeDtypeStruct((n, RET_HEADS, HEAD_DIM, HEAD_DIM), F32)],
        compiler_params=_params("parallel"),
        name="retention_step",
    )(proj, proj, proj, proj, cos2, sin2,
      jnp.broadcast_to(log_gamma[:, None, None], (RET_HEADS, 1, HEAD_DIM)), state)


def _rotary_tables(pos):
    inv = ROPE_BASE ** (-jnp.arange(0, HEAD_DIM, 2, dtype=F32) / HEAD_DIM)
    ang = pos.astype(F32)[:, None] * inv[None, :]
    cos, sin = jnp.cos(ang), jnp.sin(ang)
    return jnp.concatenate([cos, cos], axis=-1), jnp.concatenate([-sin, sin], axis=-1)


def _tiles(m):
    return {"tm": min(m, 512), "tm_wide": min(m, 1024), "tm_down": min(m, 256), "tn_in": IN_COLS // 4, "tf": 512}


def _log_gamma():
    return jnp.log1p(-jnp.exp2(-5.0 - jnp.arange(RET_HEADS, dtype=F32)))


def _mix_prompt(proj, b, seq, p, layer, depth, kv_all):
    proj3 = proj.reshape(b, seq, IN_COLS)
    conv_out, conv_new = conv_prompt(proj3, jnp.zeros((b, CONV_WIDTH - 1, CONV_DIM), F32), p["conv_w"])
    sb_out, kv_all = sb_prompt(proj3, p["sb_bias"], p["sb_gain"], layer, depth, kv_all)
    cos2, sin2 = _rotary_tables(jnp.arange(seq, dtype=jnp.int32))
    ret_out, ret_new = retention(proj3, cos2, sin2, _log_gamma(),
                                 jnp.zeros((b, RET_HEADS, HEAD_DIM, HEAD_DIM), F32))
    mix = (conv_out.reshape(b * seq, CONV_DIM), sb_out.reshape(b * seq, SB_DIM), ret_out.reshape(b * seq, RET_DIM))
    return mix, kv_all, conv_new, ret_new


def _mix_sample(proj, p, cache_k, cache_v, layer, state_conv, state_ret, page_table):
    m = proj.shape[0]
    conv_out, conv_new = conv_step(proj, state_conv, p["conv_w"])
    heads = lambda col: proj[:, col * HEAD_DIM:(col + SB_HEADS) * HEAD_DIM].reshape(m, SB_HEADS, HEAD_DIM)
    sb_out = sb_decode(heads(COL_SQ), cache_k, cache_v, layer, page_table, p["sb_bias"], p["sb_gain"])
    past_len = page_table.shape[1] * PAGE_SIZE
    cos2, sin2 = _rotary_tables(jnp.full((1,), past_len, jnp.int32))
    ret_out, ret_new = retention_step(proj, cos2, sin2, _log_gamma(), state_ret)
    mix = (conv_out, sb_out.reshape(m, SB_DIM).astype(BF16), ret_out.astype(BF16))
    return mix, heads(COL_SK)[:, None], heads(COL_SV)[:, None], conv_new, ret_new


def kernel(x_prompt, x_sample, cache_k, cache_v, state_conv, state_ret, page_table, norm_mix_pre, norm_mix_post, norm_ffn_pre, norm_ffn_post, w_in, conv_w, sb_bias, sb_gain, w_out, w_gate_up, w_down):
    b, seq, _ = x_prompt.shape
    n = x_sample.shape[0]
    depth = w_in.shape[0]
    xp = x_prompt.reshape(b * seq, D_MODEL)
    xs = x_sample.reshape(n, D_MODEL)
    xpn, xsn = xp, xs
    kv_all = None
    outs_p, outs_s = [], []
    w_out_bf, w_down_bf = w_out.astype(BF16), w_down.astype(BF16)
    t = _tiles(b * seq)
    for l in range(depth):
        p = {"conv_w": conv_w[l], "sb_bias": sb_bias[l], "sb_gain": sb_gain[l],
             "g_mix_post": norm_mix_post[l], "g_ffn_pre": norm_ffn_pre[l], "g_ffn_post": norm_ffn_post[l]}
        g_next = norm_mix_pre[l + 1] if l + 1 < depth else None
        proj_p, proj_s = in_proj(xpn, xsn, w_in, l, t["tm"], t["tn_in"], norm_mix_pre[0] if l == 0 else None)
        mix_p, kv_all, *rest_p = _mix_prompt(proj_p, b, seq, p, l, depth, kv_all)
        mix_s, *rest_s = _mix_sample(proj_s, p, cache_k, cache_v, l, state_conv[l], state_ret[l], page_table)
        xp, xpn, xs, xsn = out_proj(mix_p, xp, mix_s, xs, w_out_bf, l, p["g_mix_post"], p["g_ffn_pre"], t["tm"])
        hp, hs = gate_up(xpn, xsn, w_gate_up, l, t["tm_wide"], t["tf"])
        xp, xs, xpn, xsn = down_proj(hp, xp, hs, xs, w_down_bf, l, p["g_ffn_post"], g_next, t["tm_down"])
        outs_p.append(rest_p)
        outs_s.append(rest_s)
    stack = lambda outs, i: jnp.stack([o[i] for o in outs])
    k_prompt, v_prompt = (a.reshape(depth, b, seq, SB_HEADS, HEAD_DIM) for a in kv_all)
    return (xp.reshape(b, seq, D_MODEL), xs.reshape(n, 1, D_MODEL),
            k_prompt, v_prompt, stack(outs_p, 0), stack(outs_p, 1),
            stack(outs_s, 0), stack(outs_s, 1), stack(outs_s, 2), stack(outs_s, 3))
```
